```python
import math
import jax, jax.numpy as jnp
from jax import lax
import numpy as np

D_MODEL = 1024
BATCH = 8
SEQ = 2048
DEPTH = 1
DEC_BATCH = 1
DEC_SEQ = 16384
PAST_LEN = 128

N_META = 16
MIX_WIDTH = D_MODEL
RWKV_WIDTH = MIX_WIDTH // 2
RWKV_HEAD = 64
RWKV_HEADS = RWKV_WIDTH // RWKV_HEAD
DIFF_WIDTH = MIX_WIDTH - RWKV_WIDTH
DIFF_HEAD = 64
DIFF_HEADS = DIFF_WIDTH // (2 * DIFF_HEAD)
DECAY_LORA = 64
ICLR_LORA = 64
GATE_LORA = 128
D_FF = -(-8 * D_MODEL // (3 * 256)) * 256
ROPE_THETA = 10000.0
Q_BLOCK = 128
RMS_EPS = 1e-6
GN_EPS = 64e-5
RWKV_SPLITS = (RWKV_WIDTH, RWKV_WIDTH, RWKV_WIDTH, GATE_LORA, DECAY_LORA, DECAY_LORA, ICLR_LORA, ICLR_LORA)
SHIFT_COLS = sum(RWKV_SPLITS)
IN_COLS = SHIFT_COLS + 3 * DIFF_WIDTH

kernel_name = 'hymba_rwkv7_diffattn_encoder'


def _split(t, sizes):
    idx = [int(i) for i in np.cumsum(sizes)[:-1]]
    return jnp.split(t, idx, axis=-1)


def _rms_norm(x, g):
    xf = x.astype(jnp.float32)
    y = xf * lax.rsqrt(jnp.mean(xf * xf, axis=-1, keepdims=True) + RMS_EPS)
    return (y * g.astype(jnp.float32)).astype(x.dtype)


def _centred_shift(z, mu_prev, mu_next):
    z_prev = jnp.pad(z, ((0, 0), (1, 0), (0, 0)))[:, :-1]
    z_next = jnp.pad(z, ((0, 0), (0, 1), (0, 0)))[:, 1:]
    return z + mu_prev * (z_prev - z) + mu_next * (z_next - z)


def _wkv7_scan(r, w, k, v, a, b, inclusive):
    bsz, _, h, n = r.shape
    xs = tuple(jnp.moveaxis(t, 1, 0) for t in (r, w, k, v, a, b))

    def step(state, inp):
        r_t, w_t, k_t, v_t, a_t, b_t = inp
        sa = jnp.einsum('bhvk,bhk->bhv', state, a_t)
        new = state * w_t[:, :, None, :] + sa[..., None] * b_t[:, :, None, :] + v_t[..., None] * k_t[:, :, None, :]
        y = jnp.einsum('bhvk,bhk->bhv', new if inclusive else state, r_t)
        return new, y

    s0 = jnp.zeros((bsz, h, n, n), jnp.float32)
    _, ys = lax.scan(step, s0, xs)
    return jnp.moveaxis(ys, 0, 1)


def _rwkv7_time_mix(z, w0_f, w2_f, w0_b, w2_b, a0_f, a2_f, a0_b, a2_b, g2, k_k, k_a, r_k, ln_x_w, ln_x_b):
    bsz, length, _ = z.shape
    f32 = lambda t: t.astype(jnp.float32)
    r, k, v, gd, wdf, wdb, adf, adb = _split(f32(z), RWKV_SPLITS)
    heads = lambda t: t.reshape(bsz, length, RWKV_HEADS, RWKV_HEAD)
    flip = lambda t: jnp.flip(t, axis=1)

    def decay(wd, w0, w2):
        wl = f32(w0) + jnp.tanh(wd) @ f32(w2)
        return jnp.exp(-jnp.exp(-jax.nn.softplus(-wl) - 0.5))

    def iclr(ad, a0, a2):
        return jax.nn.sigmoid(f32(a0) + ad @ f32(a2))

    g = jax.nn.sigmoid(gd) @ f32(g2)
    kk = heads(k * f32(k_k))
    kk = kk * lax.rsqrt(jnp.maximum(jnp.sum(kk * kk, axis=-1, keepdims=True), 1e-24))
    r_h, v_h = heads(r), heads(v)

    a_f = iclr(adf, a0_f, a2_f)
    a_b = iclr(adb, a0_b, a2_b)
    k_f = heads(k * (1.0 + (a_f - 1.0) * f32(k_a)))
    k_b = heads(k * (1.0 + (a_b - 1.0) * f32(k_a)))
    w_f = heads(decay(wdf, w0_f, w2_f))
    w_b = heads(decay(wdb, w0_b, w2_b))

    y_f = _wkv7_scan(r_h, w_f, k_f, v_h, -kk, kk * heads(a_f), inclusive=True)
    y_b = flip(_wkv7_scan(flip(r_h), flip(w_b), flip(k_b), flip(v_h), flip(-kk), flip(kk * heads(a_b)), inclusive=False))
    y = y_f + y_b

    mu = jnp.mean(y, axis=-1, keepdims=True)
    var = jnp.mean(jnp.square(y - mu), axis=-1, keepdims=True)
    y = ((y - mu) * lax.rsqrt(var + GN_EPS)).reshape(bsz, length, RWKV_WIDTH) * f32(ln_x_w) + f32(ln_x_b)
    bonus = (jnp.sum(r_h * k_f * f32(r_k), axis=-1, keepdims=True) * v_h).reshape(bsz, length, RWKV_WIDTH)
    return ((y + bonus) * g).astype(z.dtype)


def _rope_tables(length):
    pos = jnp.arange(length, dtype=jnp.float32)
    inv = ROPE_THETA ** (-jnp.arange(0, DIFF_HEAD, 2, dtype=jnp.float32) / DIFF_HEAD)
    ang = pos[:, None] * inv[None, :]
    ang = jnp.concatenate([ang, ang], axis=-1)
    return jnp.cos(ang), jnp.sin(ang)


def _apply_rope(t, cos, sin):
    half = DIFF_HEAD // 2
    rot = jnp.concatenate([-t[..., half:], t[..., :half]], axis=-1)
    return t * cos[None, :, None, None, :] + rot * sin[None, :, None, None, :]


def _diff_attention(zq, zk, zv, q_norm_g, k_norm_g, lam_q1, lam_k1, lam_q2, lam_k2, subln_g, lam_init):
    bsz, length, _ = zq.shape
    f32 = lambda t: t.astype(jnp.float32)
    q = f32(zq).reshape(bsz, length, DIFF_HEADS, 2, DIFF_HEAD)
    k = f32(zk).reshape(bsz, length, DIFF_HEADS, 2, DIFF_HEAD)
    v = f32(zv).reshape(bsz, length, DIFF_HEADS, 2 * DIFF_HEAD)
    cos, sin = _rope_tables(length)
    q = _apply_rope(_rms_norm(q, q_norm_g), cos, sin)
    k = _apply_rope(_rms_norm(k, k_norm_g), cos, sin)
    lam = (jnp.exp(jnp.sum(f32(lam_q1) * f32(lam_k1))) - jnp.exp(jnp.sum(f32(lam_q2) * f32(lam_k2))) + lam_init)

    n_blk = -(-length // Q_BLOCK)
    pad = n_blk * Q_BLOCK - length
    qp = jnp.pad(q, ((0, 0), (0, pad), (0, 0), (0, 0), (0, 0))) * (1.0 / math.sqrt(DIFF_HEAD))
    qb = jnp.moveaxis(qp.reshape(bsz, n_blk, Q_BLOCK, DIFF_HEADS, 2, DIFF_HEAD), 1, 0)

    def attend(q_blk):
        s = jnp.einsum('bqhcd,bkhcd->bhcqk', q_blk, k)
        p = jax.nn.softmax(s, axis=-1)
        att = p[:, :, 0] - lam * p[:, :, 1]
        return jnp.einsum('bhqk,bkhe->bqhe', att, v)

    o = lax.map(attend, qb)
    o = jnp.moveaxis(o, 0, 1).reshape(bsz, n_blk * Q_BLOCK, DIFF_HEADS, 2 * DIFF_HEAD)[:, :length]
    o = _rms_norm(o, subln_g) * (1.0 - lam_init)
    return o.reshape(bsz, length, DIFF_WIDTH).astype(zq.dtype)


def _layer(x, norm1_g, w_in, shift_mu_prev, shift_mu_next, w0_f, w2_f, w0_b, w2_b, a0_f, a2_f, a0_b, a2_b,
           g2, k_k, k_a, r_k, ln_x_w, ln_x_b, q_norm_g, k_norm_g, lam_q1, lam_k1, lam_q2, lam_k2, subln_g,
           w_out, norm2_g, w_gate, w_up, w_down, lam_init):
    h = _rms_norm(x, norm1_g)
    proj = h @ w_in
    z_rwkv = _centred_shift(proj[..., :SHIFT_COLS], shift_mu_prev, shift_mu_next)
    zq, zk, zv = _split(proj[..., SHIFT_COLS:], (DIFF_WIDTH, DIFF_WIDTH, DIFF_WIDTH))
    o_rwkv = _rwkv7_time_mix(z_rwkv, w0_f, w2_f, w0_b, w2_b, a0_f, a2_f, a0_b, a2_b, g2, k_k, k_a, r_k, ln_x_w, ln_x_b)
    o_diff = _diff_attention(zq, zk, zv, q_norm_g, k_norm_g, lam_q1, lam_k1, lam_q2, lam_k2, subln_g, lam_init)
    x = x + jnp.concatenate([o_rwkv, o_diff], axis=-1) @ w_out
    h2 = _rms_norm(x, norm2_g)
    return x + (jax.nn.silu(h2 @ w_gate) * (h2 @ w_up)) @ w_down


def _trunk(x, meta_tokens, layer_params):
    bsz = x.shape[0]
    meta = jnp.broadcast_to(meta_tokens.astype(x.dtype)[None], (bsz, N_META, D_MODEL))
    x = jnp.concatenate([meta, x], axis=1)
    for l in range(DEPTH):
        lam_init = 0.8 - 0.6 * math.exp(-0.3 * l)
        x = _layer(x, *[p[l] for p in layer_params], lam_init=lam_init)
    return x[:, N_META:]


def setup_inputs(seed: int = 0) -> dict:
    key = jax.random.key(seed)
    ks = iter(jax.random.split(key, 40))

    def nrm(shape, scale):
        return scale * jax.random.normal(next(ks), shape, jnp.float32)

    def gain(shape):
        return 1.0 + nrm(shape, 0.02)

    def unif(shape, lo, hi):
        return jax.random.uniform(next(ks), shape, jnp.float32, lo, hi)

    L = DEPTH
    C = RWKV_WIDTH
    return {
        'x_prompt': nrm((BATCH, SEQ, D_MODEL), 1.0),
        'x_sample': nrm((DEC_BATCH, DEC_SEQ, D_MODEL), 1.0),
        'meta_tokens': nrm((N_META, D_MODEL), 1.0),
        'norm1_g': gain((L, D_MODEL)),
        'w_in': nrm((L, D_MODEL, IN_COLS), D_MODEL ** -0.5),
        'shift_mu_prev': unif((L, SHIFT_COLS), 0.0, 0.5),
        'shift_mu_next': unif((L, SHIFT_COLS), 0.0, 0.5),
        'w0_f': unif((L, C), -6.0, 0.0),
        'w2_f': nrm((L, DECAY_LORA, C), 0.1 * DECAY_LORA ** -0.5),
        'w0_b': unif((L, C), -6.0, 0.0),
        'w2_b': nrm((L, DECAY_LORA, C), 0.1 * DECAY_LORA ** -0.5),
        'a0_f': nrm((L, C), 0.5),
        'a2_f': nrm((L, ICLR_LORA, C), 0.5 * ICLR_LORA ** -0.5),
        'a0_b': nrm((L, C), 0.5),
        'a2_b': nrm((L, ICLR_LORA, C), 0.5 * ICLR_LORA ** -0.5),
        'g2': nrm((L, GATE_LORA, C), GATE_LORA ** -0.5),
        'k_k': 0.85 + nrm((L, C), 0.05),
        'k_a': 1.0 + nrm((L, C), 0.05),
        'r_k': nrm((L, RWKV_HEADS, RWKV_HEAD), 0.1),
        'ln_x_w': gain((L, C)),
        'ln_x_b': nrm((L, C), 0.02),
        'q_norm_g': gain((L, DIFF_HEAD)),
        'k_norm_g': gain((L, DIFF_HEAD)),
        'lam_q1': nrm((L, DIFF_HEAD), 0.1),
        'lam_k1': nrm((L, DIFF_HEAD), 0.1),
        'lam_q2': nrm((L, DIFF_HEAD), 0.1),
        'lam_k2': nrm((L, DIFF_HEAD), 0.1),
        'subln_g': gain((L, 2 * DIFF_HEAD)),
        'w_out': nrm((L, MIX_WIDTH, D_MODEL), MIX_WIDTH ** -0.5),
        'norm2_g': gain((L, D_MODEL)),
        'w_gate': nrm((L, D_MODEL, D_FF), D_MODEL ** -0.5),
        'w_up': nrm((L, D_MODEL, D_FF), D_MODEL ** -0.5),
        'w_down': nrm((L, D_FF, D_MODEL), D_FF ** -0.5),
    }


def reference(x_prompt, x_sample, meta_tokens, norm1_g, w_in, shift_mu_prev, shift_mu_next, w0_f, w2_f, w0_b, w2_b,
              a0_f, a2_f, a0_b, a2_b, g2, k_k, k_a, r_k, ln_x_w, ln_x_b, q_norm_g, k_norm_g, lam_q1, lam_k1,
              lam_q2, lam_k2, subln_g, w_out, norm2_g, w_gate, w_up, w_down):
    layer_params = (norm1_g, w_in, shift_mu_prev, shift_mu_next, w0_f, w2_f, w0_b, w2_b, a0_f, a2_f, a0_b, a2_b,
                    g2, k_k, k_a, r_k, ln_x_w, ln_x_b, q_norm_g, k_norm_g, lam_q1, lam_k1, lam_q2, lam_k2, subln_g,
                    w_out, norm2_g, w_gate, w_up, w_down)
    y_prompt = _trunk(x_prompt, meta_tokens, layer_params)
    y_sample = _trunk(x_sample, meta_tokens, layer_params)
    return (y_prompt, y_sample)
```

```python
import functools
import math

import jax
import jax.numpy as jnp
from jax import lax
from jax.experimental import pallas as pl
from jax.experimental.pallas import tpu as pltpu

F32 = jnp.float32
BF16 = jnp.bfloat16

D_MODEL = 1024
N_META = 16
RWKV_WIDTH = 512
RWKV_HEAD = 64
DIFF_WIDTH = 512
DIFF_HEAD = 64
DIFF_HEADS = 4
N_MAPS = 2 * DIFF_HEADS
GATE_LORA = 128
LORA_PAIR = 128
SHIFT_COLS = 3 * RWKV_WIDTH + GATE_LORA + 2 * LORA_PAIR
IN_COLS = SHIFT_COLS + 3 * DIFF_WIDTH
D_FF = 2816
ROPE_THETA = 10000.0
RMS_EPS = 1e-6
GN_EPS = 64e-5
CHUNK = 64
SUB = 16
PAIR = 128
N_PAIRS = RWKV_WIDTH // PAIR
META_PAD = 128
VMEM_LIMIT_BYTES = 56 * 1024 * 1024
NEG_BIG = -1e30
HIGHEST = lax.Precision.HIGHEST


def _cparams(sem):
    return pltpu.CompilerParams(dimension_semantics=sem, vmem_limit_bytes=VMEM_LIMIT_BYTES)


def _const_spec(shape):
    nd = len(shape)
    return pl.BlockSpec(shape, lambda *_: (0,) * nd, pipeline_mode=pl.Buffered(1))


def _mm(a, b):
    return jnp.dot(a.astype(BF16), b.astype(BF16), preferred_element_type=F32)


def _mm_nt(a, b):
    return lax.dot_general(a.astype(BF16), b.astype(BF16), (((1,), (1,)), ((), ())),
                           preferred_element_type=F32)


def _split_dot(x, m, terms):
    acc = None
    rem = x
    for _ in range(terms):
        piece = rem.astype(BF16)
        part = jnp.dot(piece, m, preferred_element_type=F32)
        acc = part if acc is None else acc + part
        rem = rem - piece.astype(F32)
    return acc


def _tri_dot(tri, x):
    acc = None
    rem = x
    for _ in range(3):
        piece = rem.astype(BF16)
        part = jnp.dot(tri, piece, preferred_element_type=F32)
        acc = part if acc is None else acc + part
        rem = rem - piece.astype(F32)
    return acc


def _inproj_kernel(x_ref, g1_ref, w_ref, cos_ref, sin_ref, qg_ref, kg_ref, gs_ref,
                   p_ref, q_ref, kt_ref, v_ref):
    x = x_ref[0]
    ms = jnp.mean(x * x, axis=-1, keepdims=True)
    h = (x * lax.rsqrt(ms + RMS_EPS) * g1_ref[...]).astype(BF16)
    p_ref[0] = jnp.dot(h, w_ref[:, :SHIFT_COLS], preferred_element_type=F32)
    qkv = jnp.dot(h, w_ref[:, SHIFT_COLS:], preferred_element_type=F32)
    q = qkv[:, :DIFF_WIDTH]
    k = qkv[:, DIFF_WIDTH:2 * DIFF_WIDTH]
    v = qkv[:, 2 * DIFF_WIDTH:]

    reps = DIFF_WIDTH // cos_ref.shape[1]
    cos = jnp.tile(cos_ref[...], (1, reps))
    sin = jnp.tile(sin_ref[...], (1, reps))
    lane = lax.broadcasted_iota(jnp.int32, q.shape, 1)
    first_half = (lane % DIFF_HEAD) < (DIFF_HEAD // 2)
    gs = gs_ref[...]

    def norm_rope(t, g):
        ss = _split_dot(t * t, gs, 2)
        t = t * lax.rsqrt(ss * (1.0 / DIFF_HEAD) + RMS_EPS) * g
        rot = jnp.where(first_half,
                        pltpu.roll(t, DIFF_WIDTH - DIFF_HEAD // 2, 1),
                        pltpu.roll(t, DIFF_HEAD // 2, 1))
        return t * cos + rot * sin

    qr = norm_rope(q, qg_ref[...]) * (1.0 / math.sqrt(DIFF_HEAD))
    kr = norm_rope(k, kg_ref[...])
    krt = kr.T
    for j in range(N_MAPS):
        q_ref[0, j] = qr[:, j * DIFF_HEAD:(j + 1) * DIFF_HEAD].astype(BF16)
        kt_ref[0, j] = krt[j * DIFF_HEAD:(j + 1) * DIFF_HEAD, :].astype(BF16)
    for hd in range(DIFF_HEADS):
        v_ref[0, hd] = v[:, hd * PAIR:(hd + 1) * PAIR].astype(BF16)


def _inproj(x, cos, sin, g1, w_in, qg, kg, gsum, tm):
    bsz, length, _ = x.shape
    grid = (bsz, length // tm)
    return pl.pallas_call(
        _inproj_kernel,
        grid=grid,
        in_specs=[
            pl.BlockSpec((1, tm, D_MODEL), lambda b, i: (b, i, 0)),
            _const_spec((1, D_MODEL)),
            _const_spec((D_MODEL, IN_COLS)),
            pl.BlockSpec((tm, PAIR), lambda b, i: (i, 0)),
            pl.BlockSpec((tm, PAIR), lambda b, i: (i, 0)),
            _const_spec((1, DIFF_WIDTH)),
            _const_spec((1, DIFF_WIDTH)),
            _const_spec((DIFF_WIDTH, DIFF_WIDTH)),
        ],
        out_specs=[
            pl.BlockSpec((1, tm, SHIFT_COLS), lambda b, i: (b, i, 0)),
            pl.BlockSpec((1, N_MAPS, tm, DIFF_HEAD), lambda b, i: (b, 0, i, 0)),
            pl.BlockSpec((1, N_MAPS, DIFF_HEAD, tm), lambda b, i: (b, 0, 0, i)),
            pl.BlockSpec((1, DIFF_HEADS, tm, PAIR), lambda b, i: (b, 0, i, 0)),
        ],
        out_shape=[
            jax.ShapeDtypeStruct((bsz, length, SHIFT_COLS), F32),
            jax.ShapeDtypeStruct((bsz, N_MAPS, length, DIFF_HEAD), BF16),
            jax.ShapeDtypeStruct((bsz, N_MAPS, DIFF_HEAD, length), BF16),
            jax.ShapeDtypeStruct((bsz, DIFF_HEADS, length, PAIR), BF16),
        ],
        compiler_params=_cparams(("parallel", "parallel")),
        name="inproj",
    )(x, g1, w_in, cos, sin, qg, kg, gsum)


def _prep_kernel(p_ref, ph_ref, nh_ref, p0_ref, mup_ref, mun_ref, w0_ref, w2_ref, a0_ref, a2_ref,
                 g2_ref, kk_ref, ka_ref, rk_ref, gs_ref,
                 raf_ref, kbf_ref, bktf_ref, wcf_ref, rab_ref, kbb_ref, bktb_ref, wcb_ref,
                 v_ref, g_ref, bonus_ref, *, valid_rows):
    i = pl.program_id(1)
    last = pl.num_programs(1) - 1
    p = p_ref[0]
    tm = p.shape[0]
    row = lax.broadcasted_iota(jnp.int32, p.shape, 0)
    prev_row = jnp.where(i == 0, p0_ref[7:8, :], ph_ref[0, 7:8, :])
    next_row = jnp.where(i == last, jnp.zeros_like(prev_row), nh_ref[0, 0:1, :])
    p_prev = jnp.where(row == 0, prev_row, pltpu.roll(p, 1, 0))
    p_next = jnp.where(row == tm - 1, next_row, pltpu.roll(p, tm - 1, 0))
    z = p + mup_ref[...] * (p_prev - p) + mun_ref[...] * (p_next - p)

    w = RWKV_WIDTH
    r = z[:, :w]
    k = z[:, w:2 * w]
    v = z[:, 2 * w:3 * w]
    gd = z[:, 3 * w:3 * w + GATE_LORA]
    wd = z[:, 3 * w + GATE_LORA:3 * w + GATE_LORA + LORA_PAIR]
    ad = z[:, 3 * w + GATE_LORA + LORA_PAIR:]

    wl = w0_ref[...] + jnp.dot(jnp.tanh(wd), w2_ref[...], preferred_element_type=F32, precision=HIGHEST)
    lw = (-math.exp(-0.5)) * jax.nn.sigmoid(wl)
    al = a0_ref[...] + jnp.dot(ad, a2_ref[...], preferred_element_type=F32, precision=HIGHEST)
    iclr = jax.nn.sigmoid(al)
    g = _mm(jax.nn.sigmoid(gd), g2_ref[...])

    gs = gs_ref[...]
    kk = k * kk_ref[...]
    kk = kk * lax.rsqrt(jnp.maximum(_split_dot(kk * kk, gs, 2), 1e-24))
    k_a = ka_ref[...]
    a_f = iclr[:, :w]
    a_b = iclr[:, w:]
    k_f = k * (1.0 + (a_f - 1.0) * k_a)
    k_b = k * (1.0 + (a_b - 1.0) * k_a)
    bonus = _split_dot(r * k_f * rk_ref[...], gs, 2) * v
    a_neg = -kk
    b_f = kk * a_f
    b_b = kk * a_b
    lw_f = lw[:, :w]
    lw_b = lw[:, w:]

    if valid_rows is not None:
        keep = lax.broadcasted_iota(jnp.int32, r.shape, 0) < valid_rows
        zero = jnp.zeros_like(r)
        r, v, a_neg = (jnp.where(keep, t, zero) for t in (r, v, a_neg))
        k_f, k_b, b_f, b_b = (jnp.where(keep, t, zero) for t in (k_f, k_b, b_f, b_b))
        lw_f, lw_b = (jnp.where(keep, t, zero) for t in (lw_f, lw_b))

    v_ref[0] = v.astype(BF16)
    g_ref[0] = g.astype(BF16)
    bonus_ref[0] = bonus.astype(BF16)

    ti = lax.broadcasted_iota(jnp.int32, (CHUNK, CHUNK), 0)
    si = lax.broadcasted_iota(jnp.int32, (CHUNK, CHUNK), 1)
    tri_f = jnp.where(si <= ti, 1.0, 0.0).astype(BF16)
    tri_b = jnp.where(si >= ti, 1.0, 0.0).astype(BF16)

    for c in range(tm // CHUNK):
        rows = slice(c * CHUNK, (c + 1) * CHUNK)
        r_c, v_c, a_c = r[rows], v[rows], a_neg[rows]
        for (lw_d, k_d, b_d, tri, fwd, ra_ref, kb_ref, bkt_ref, wc_ref) in (
                (lw_f, k_f, b_f, tri_f, True, raf_ref, kbf_ref, bktf_ref, wcf_ref),
                (lw_b, k_b, b_b, tri_b, False, rab_ref, kbb_ref, bktb_ref, wcb_ref)):
            lw_c = lw_d[rows]
            cum = _tri_dot(tri, lw_c)
            tot = cum[CHUNK - 1:CHUNK] if fwd else cum[0:1]
            e_excl = jnp.exp(cum - lw_c)
            e_read = jnp.exp(cum) if fwd else e_excl
            e_inv = jnp.exp(-cum)
            e_end = jnp.exp(tot - cum)
            ra_ref[0, 2 * c * CHUNK:(2 * c + 1) * CHUNK] = (r_c * e_read).astype(BF16)
            ra_ref[0, (2 * c + 1) * CHUNK:(2 * c + 2) * CHUNK] = (a_c * e_excl).astype(BF16)
            kb_ref[0, 2 * c * CHUNK:(2 * c + 1) * CHUNK] = (k_d[rows] * e_inv).astype(BF16)
            kb_ref[0, (2 * c + 1) * CHUNK:(2 * c + 2) * CHUNK] = (b_d[rows] * e_inv).astype(BF16)
            bk = jnp.concatenate([b_d[rows] * e_end, k_d[rows] * e_end], axis=0)
            bkt_ref[0, c * w:(c + 1) * w] = bk.T.astype(BF16)
            wc_ref[0, 8 * c:8 * (c + 1)] = jnp.broadcast_to(jnp.exp(tot), (8, w))


def _prep(p, prev8, consts, tm, valid_rows):
    bsz, length, _ = p.shape
    nt = length // tm
    nc = length // CHUNK
    cpt = tm // CHUNK
    hb = tm // 8
    n8 = length // 8
    mup, mun, w0, w2, a0, a2, g2, k_k, k_a, r_k, gsum = consts
    tok = lambda b, i: (b, i, 0)
    dir_specs = [
        pl.BlockSpec((1, 2 * tm, RWKV_WIDTH), tok),
        pl.BlockSpec((1, 2 * tm, RWKV_WIDTH), tok),
        pl.BlockSpec((1, cpt * RWKV_WIDTH, PAIR), tok),
        pl.BlockSpec((1, cpt * 8, RWKV_WIDTH), tok),
    ]
    dir_shapes = [
        jax.ShapeDtypeStruct((bsz, 2 * length, RWKV_WIDTH), BF16),
        jax.ShapeDtypeStruct((bsz, 2 * length, RWKV_WIDTH), BF16),
        jax.ShapeDtypeStruct((bsz, nc * RWKV_WIDTH, PAIR), BF16),
        jax.ShapeDtypeStruct((bsz, nc * 8, RWKV_WIDTH), F32),
    ]
    tok_spec = pl.BlockSpec((1, tm, RWKV_WIDTH), tok)
    tok_shape = jax.ShapeDtypeStruct((bsz, length, RWKV_WIDTH), BF16)
    return pl.pallas_call(
        functools.partial(_prep_kernel, valid_rows=valid_rows),
        grid=(bsz, nt),
        in_specs=[
            pl.BlockSpec((1, tm, SHIFT_COLS), tok),
            pl.BlockSpec((1, 8, SHIFT_COLS), lambda b, i: (b, jnp.maximum(i * hb - 1, 0), 0)),
            pl.BlockSpec((1, 8, SHIFT_COLS), lambda b, i: (b, jnp.minimum((i + 1) * hb, n8 - 1), 0)),
            _const_spec((8, SHIFT_COLS)),
            _const_spec((1, SHIFT_COLS)),
            _const_spec((1, SHIFT_COLS)),
            _const_spec((1, 2 * RWKV_WIDTH)),
            _const_spec((LORA_PAIR, 2 * RWKV_WIDTH)),
            _const_spec((1, 2 * RWKV_WIDTH)),
            _const_spec((LORA_PAIR, 2 * RWKV_WIDTH)),
            _const_spec((GATE_LORA, RWKV_WIDTH)),
            _const_spec((1, RWKV_WIDTH)),
            _const_spec((1, RWKV_WIDTH)),
            _const_spec((1, RWKV_WIDTH)),
            _const_spec((RWKV_WIDTH, RWKV_WIDTH)),
        ],
        out_specs=dir_specs + dir_specs + [tok_spec, tok_spec, tok_spec],
        out_shape=dir_shapes + dir_shapes + [tok_shape, tok_shape, tok_shape],
        compiler_params=_cparams(("parallel", "parallel")),
        name="rwkv_prep",
    )(p, p, p, prev8, mup, mun, w0, w2, a0, a2, g2, k_k, k_a, r_k, gsum)


def _row_blocks(x, lane_lo):
    zero = jnp.zeros_like(x)
    return jnp.concatenate([jnp.where(lane_lo, x, zero), jnp.where(lane_lo, zero, x)], axis=0)


def _scan_direction(ra_ref, kb_ref, bkt_ref, wc_ref, v_ref, y_ref, s_ref, fwd):
    ti = lax.broadcasted_iota(jnp.int32, (CHUNK, PAIR), 0)
    li = lax.broadcasted_iota(jnp.int32, (CHUNK, PAIR), 1)
    si = li % CHUNK
    lane_lo = li < CHUNK
    if fwd:
        mask_a = ti > si
        mask_r = ti >= si
    else:
        mask_a = ti < si
        mask_r = mask_a
    same_sub = (ti // SUB) == (si // SUB)
    eye_w = jnp.where(ti == si, 1.0, 0.0).astype(F32)
    rr = lax.broadcasted_iota(jnp.int32, (PAIR, PAIR), 0)
    cc = lax.broadcasted_iota(jnp.int32, (PAIR, PAIR), 1)
    same_head = (rr // RWKV_HEAD) == (cc // RWKV_HEAD)
    diag = rr == cc
    rb = lambda t: _row_blocks(t, lane_lo)

    for j in range(N_PAIRS):
        sl = slice(j * PAIR, (j + 1) * PAIR)
        ra = ra_ref[0, :, sl]
        kb = kb_ref[0, :, sl]
        v2 = v_ref[0, :, sl]
        rt2, at2 = ra[:CHUNK], ra[CHUNK:]
        ak = _mm_nt(ra, rb(kb[:CHUNK]))
        ab = _mm_nt(ra, rb(kb[CHUNK:]))
        a_rk = jnp.where(mask_r, ak[:CHUNK], 0.0)
        a_ak = jnp.where(mask_a, ak[CHUNK:], 0.0)
        a_rb = jnp.where(mask_r, ab[:CHUNK], 0.0)
        n_all = jnp.where(mask_a, ab[CHUNK:], 0.0)

        d1 = jnp.where(same_sub, n_all, 0.0)
        e1 = n_all - d1
        d2 = _mm(d1, rb(d1))
        d4 = _mm(d2, rb(d2))
        d8 = _mm(d4, rb(d4))
        t16 = eye_w + d1
        t16 = t16 + _mm(t16, rb(d2))
        t16 = t16 + _mm(t16, rb(d4))
        t16 = t16 + _mm(t16, rb(d8))
        m1 = _mm(t16, rb(e1))
        m2 = _mm(m1, rb(m1))
        zz = eye_w + m1
        zz = zz + _mm(zz, rb(m2))
        tinv = _mm(zz, rb(t16))

        rbv = rb(v2)
        av = _mm(a_ak, rbv)
        tu = _mm(tinv, jnp.concatenate([rb(at2), rb(av.astype(BF16))], axis=1))
        ap, u0 = tu[:, :PAIR], tu[:, PAIR:]
        ar = _mm(a_rb, jnp.concatenate([rb(ap), rb(u0)], axis=1))
        rp = rt2.astype(F32) + ar[:, :PAIR]
        y0 = ar[:, PAIR:] + _mm(a_rk, rbv)
        stack = jnp.concatenate(
            [jnp.concatenate([ap, u0], axis=1),
             jnp.concatenate([jnp.zeros_like(ap), v2.astype(F32)], axis=1)], axis=0)
        pd = _mm(bkt_ref[0, sl, :], stack)
        wc = wc_ref[0, 0:1, sl]
        p2 = jnp.where(same_head, pd[:, :PAIR], 0.0) + jnp.where(diag, wc, 0.0)
        d02 = jnp.where(same_head, pd[:, PAIR:], 0.0)
        s_old = s_ref[j]
        y_ref[0, :, sl] = _mm(rp, s_old) + y0
        s_ref[j] = jnp.dot(p2, s_old, preferred_element_type=F32, precision=HIGHEST) + d02


def _scan_kernel(raf_ref, kbf_ref, bktf_ref, wcf_ref, vf_ref,
                 rab_ref, kbb_ref, bktb_ref, wcb_ref, vb_ref, s0_ref,
                 yf_ref, yb_ref, sout_ref, sf_ref, sb_ref):
    i = pl.program_id(1)

    @pl.when(i == 0)
    def _():
        sf_ref[...] = s0_ref[0]
        sb_ref[...] = jnp.zeros_like(sb_ref)

    _scan_direction(raf_ref, kbf_ref, bktf_ref, wcf_ref, vf_ref, yf_ref, sf_ref, True)
    _scan_direction(rab_ref, kbb_ref, bktb_ref, wcb_ref, vb_ref, yb_ref, sb_ref, False)

    @pl.when(i == pl.num_programs(1) - 1)
    def _():
        sout_ref[0] = sf_ref[...]


def _scan(prep_out, s0):
    raf, kbf, bktf, wcf, rab, kbb, bktb, wcb, v = prep_out[:9]
    bsz, length, _ = v.shape
    nc = length // CHUNK
    fw = lambda b, i: (b, i, 0)
    bw = lambda b, i: (b, nc - 1 - i, 0)

    def dir_specs(im):
        return [
            pl.BlockSpec((1, 2 * CHUNK, RWKV_WIDTH), im),
            pl.BlockSpec((1, 2 * CHUNK, RWKV_WIDTH), im),
            pl.BlockSpec((1, RWKV_WIDTH, PAIR), im),
            pl.BlockSpec((1, 8, RWKV_WIDTH), im),
            pl.BlockSpec((1, CHUNK, RWKV_WIDTH), im),
        ]

    state_spec = pl.BlockSpec((1, N_PAIRS, PAIR, PAIR), lambda b, i: (b, 0, 0, 0))
    return pl.pallas_call(
        _scan_kernel,
        grid=(bsz, nc),
        in_specs=dir_specs(fw) + dir_specs(bw) + [state_spec],
        out_specs=[
            pl.BlockSpec((1, CHUNK, RWKV_WIDTH), fw),
            pl.BlockSpec((1, CHUNK, RWKV_WIDTH), bw),
            state_spec,
        ],
        out_shape=[
            jax.ShapeDtypeStruct((bsz, length, RWKV_WIDTH), F32),
            jax.ShapeDtypeStruct((bsz, length, RWKV_WIDTH), F32),
            jax.ShapeDtypeStruct((bsz, N_PAIRS, PAIR, PAIR), F32),
        ],
        scratch_shapes=[pltpu.VMEM((N_PAIRS, PAIR, PAIR), F32), pltpu.VMEM((N_PAIRS, PAIR, PAIR), F32)],
        compiler_params=_cparams(("parallel", "arbitrary")),
        name="rwkv_scan",
    )(raf, kbf, bktf, wcf, v, rab, kbb, bktb, wcb, v, s0)


def _attn_kernel(q_ref, kt_ref, v_ref, ktm_ref, vm_ref, lq1_ref, lk1_ref, lq2_ref, lk2_ref, sg_ref,
                 o_ref, m_ref, acc_ref, *, lam_init):
    kv = pl.program_id(2)
    tq = q_ref.shape[2]
    tk = kt_ref.shape[3]
    ones_tk = jnp.ones((tk, PAIR), BF16)
    ones_meta = jnp.ones((META_PAD, PAIR), BF16)

    @pl.when(kv == 0)
    def _():
        col = lax.broadcasted_iota(jnp.int32, (tq, META_PAD), 1)
        for hc in range(N_MAPS):
            s = jnp.dot(q_ref[0, hc], ktm_ref[0, hc], preferred_element_type=F32)
            s = jnp.where(col < N_META, s, NEG_BIG)
            m = jnp.max(s, axis=-1, keepdims=True)
            p = jnp.exp(s - m).astype(BF16)
            vext = jnp.concatenate([vm_ref[0, hc // 2], ones_meta], axis=1)
            acc_ref[hc] = jnp.dot(p, vext, preferred_element_type=F32)
            m_ref[hc] = jnp.broadcast_to(m, (tq, PAIR))

    for hc in range(N_MAPS):
        s = jnp.dot(q_ref[0, hc], kt_ref[0, hc], preferred_element_type=F32)
        m_prev = m_ref[hc]
        m_next = jnp.maximum(m_prev, jnp.max(s, axis=-1, keepdims=True))
        alpha = jnp.exp(m_prev - m_next)
        p = jnp.exp(s - jnp.tile(m_next, (1, tk // PAIR))).astype(BF16)
        vext = jnp.concatenate([v_ref[0, hc // 2], ones_tk], axis=1)
        acc_ref[hc] = jnp.tile(alpha, (1, 2)) * acc_ref[hc] + jnp.dot(p, vext, preferred_element_type=F32)
        m_ref[hc] = m_next

    @pl.when(kv == pl.num_programs(2) - 1)
    def _():
        lam = (jnp.exp(jnp.sum(lq1_ref[...] * lk1_ref[...], axis=-1, keepdims=True))
               - jnp.exp(jnp.sum(lq2_ref[...] * lk2_ref[...], axis=-1, keepdims=True)) + lam_init)
        for hd in range(DIFF_HEADS):
            a1 = acc_ref[2 * hd]
            a2 = acc_ref[2 * hd + 1]
            o = a1[:, :PAIR] / a1[:, PAIR:] - lam * (a2[:, :PAIR] / a2[:, PAIR:])
            ms = jnp.mean(o * o, axis=-1, keepdims=True)
            o = o * lax.rsqrt(ms + RMS_EPS) * sg_ref[...] * (1.0 - lam_init)
            o_ref[0, :, hd * PAIR:(hd + 1) * PAIR] = o.astype(BF16)


def _attention(q, kt, v, ktm, vm, lam_params, subln_g, lam_init, tq, tk):
    bsz, _, length, _ = q.shape
    lq1, lk1, lq2, lk2 = lam_params
    return pl.pallas_call(
        functools.partial(_attn_kernel, lam_init=lam_init),
        grid=(bsz, length // tq, length // tk),
        in_specs=[
            pl.BlockSpec((1, N_MAPS, tq, DIFF_HEAD), lambda b, i, j: (b, 0, i, 0)),
            pl.BlockSpec((1, N_MAPS, DIFF_HEAD, tk), lambda b, i, j: (b, 0, 0, j)),
            pl.BlockSpec((1, DIFF_HEADS, tk, PAIR), lambda b, i, j: (b, 0, j, 0)),
            _const_spec((1, N_MAPS, DIFF_HEAD, META_PAD)),
            _const_spec((1, DIFF_HEADS, META_PAD, PAIR)),
            _const_spec((1, DIFF_HEAD)),
            _const_spec((1, DIFF_HEAD)),
            _const_spec((1, DIFF_HEAD)),
            _const_spec((1, DIFF_HEAD)),
            _const_spec((1, PAIR)),
        ],
        out_specs=pl.BlockSpec((1, tq, DIFF_WIDTH), lambda b, i, j: (b, i, 0)),
        out_shape=jax.ShapeDtypeStruct((bsz, length, DIFF_WIDTH), BF16),
        scratch_shapes=[pltpu.VMEM((N_MAPS, tq, PAIR), F32), pltpu.VMEM((N_MAPS, tq, 2 * PAIR), F32)],
        compiler_params=_cparams(("parallel", "parallel", "arbitrary")),
        name="diff_attn",
    )(q, kt, v, ktm, vm, lq1, lk1, lq2, lk2, subln_g)


def _out_kernel(x_ref, yf_ref, yb_ref, bonus_ref, g_ref, od_ref, lnw_ref, lnb_ref, gs_ref,
                wo_ref, g2n_ref, wg_ref, wu_ref, wd_ref, o_ref):
    gs = gs_ref[...]
    y = yf_ref[0] + yb_ref[0]
    mu = _split_dot(y, gs, 2) * (1.0 / RWKV_HEAD)
    yc = y - mu
    var = _split_dot(yc * yc, gs, 2) * (1.0 / RWKV_HEAD)
    yn = yc * lax.rsqrt(var + GN_EPS) * lnw_ref[...] + lnb_ref[...]
    o_rwkv = (yn + bonus_ref[0].astype(F32)) * g_ref[0].astype(F32)
    x1 = (x_ref[0]
          + jnp.dot(o_rwkv.astype(BF16), wo_ref[:RWKV_WIDTH, :], preferred_element_type=F32)
          + jnp.dot(od_ref[0], wo_ref[RWKV_WIDTH:, :], preferred_element_type=F32))
    ms = jnp.mean(x1 * x1, axis=-1, keepdims=True)
    h2 = (x1 * lax.rsqrt(ms + RMS_EPS) * g2n_ref[...]).astype(BF16)
    gate = jnp.dot(h2, wg_ref[...], preferred_element_type=F32)
    up = jnp.dot(h2, wu_ref[...], preferred_element_type=F32)
    act = (gate * jax.nn.sigmoid(gate) * up).astype(BF16)
    o_ref[0] = x1 + jnp.dot(act, wd_ref[...], preferred_element_type=F32)


def _out(x, yf, yb, bonus, g, od, consts, tm):
    bsz, length, _ = x.shape
    lnw, lnb, gsum, wo, g2n, wg, wu, wd = consts
    tok = lambda b, i: (b, i, 0)
    half = pl.BlockSpec((1, tm, RWKV_WIDTH), tok)
    return pl.pallas_call(
        _out_kernel,
        grid=(bsz, length // tm),
        in_specs=[
            pl.BlockSpec((1, tm, D_MODEL), tok), half, half, half, half, half,
            _const_spec((1, RWKV_WIDTH)),
            _const_spec((1, RWKV_WIDTH)),
            _const_spec((RWKV_WIDTH, RWKV_WIDTH)),
            _const_spec((D_MODEL, D_MODEL)),
            _const_spec((1, D_MODEL)),
            _const_spec((D_MODEL, D_FF)),
            _const_spec((D_MODEL, D_FF)),
            _const_spec((D_FF, D_MODEL)),
        ],
        out_specs=pl.BlockSpec((1, tm, D_MODEL), tok),
        out_shape=jax.ShapeDtypeStruct((bsz, length, D_MODEL), F32),
        compiler_params=_cparams(("parallel", "parallel")),
        name="out_ffn",
    )(x, yf, yb, bonus, g, od, lnw, lnb, gsum, wo, g2n, wg, wu, wd)


def _rope_tables(length, offset):
    pos = jnp.arange(length, dtype=F32) + offset
    inv = ROPE_THETA ** (-jnp.arange(0, DIFF_HEAD, 2, dtype=F32) / DIFF_HEAD)
    ang = pos[:, None] * inv[None, :]
    cos = jnp.concatenate([jnp.cos(ang)] * 4, axis=-1)
    sin = jnp.sin(ang)
    sin = jnp.concatenate([-sin, sin, -sin, sin], axis=-1)
    return cos, sin


def _block_diag2(a, b):
    za = jnp.zeros_like(a)
    return jnp.concatenate([jnp.concatenate([a, za], axis=1), jnp.concatenate([za, b], axis=1)], axis=0)


def _tile_for(length, pref):
    t = pref
    while length % t:
        t //= 2
    return t


def kernel(x_prompt, x_sample, meta_tokens, norm1_g, w_in, shift_mu_prev, shift_mu_next, w0_f, w2_f, w0_b, w2_b, a0_f, a2_f, a0_b, a2_b, g2, k_k, k_a, r_k, ln_x_w, ln_x_b, q_norm_g, k_norm_g, lam_q1, lam_k1, lam_q2, lam_k2, subln_g, w_out, norm2_g, w_gate, w_up, w_down):
    lam_init = 0.8 - 0.6 * math.exp(-0.3 * 0)
    row = lambda t: t.reshape(1, -1).astype(F32)
    gi = jnp.arange(RWKV_WIDTH) // RWKV_HEAD
    gsum = (gi[:, None] == gi[None, :]).astype(BF16)

    g1 = row(norm1_g[0])
    w_in_b = w_in[0].astype(BF16)
    qg = row(jnp.tile(q_norm_g[0], N_MAPS))
    kg = row(jnp.tile(k_norm_g[0], N_MAPS))
    prep_consts = (
        row(shift_mu_prev[0]), row(shift_mu_next[0]),
        row(jnp.concatenate([w0_f[0], w0_b[0]])), _block_diag2(w2_f[0], w2_b[0]).astype(F32),
        row(jnp.concatenate([a0_f[0], a0_b[0]])), _block_diag2(a2_f[0], a2_b[0]).astype(F32),
        g2[0].astype(BF16), row(k_k[0]), row(k_a[0]), row(r_k[0]), gsum,
    )
    lam_params = (row(lam_q1[0]), row(lam_k1[0]), row(lam_q2[0]), row(lam_k2[0]))
    sg = row(subln_g[0])
    out_consts = (row(ln_x_w[0]), row(ln_x_b[0]), gsum, w_out[0].astype(BF16), row(norm2_g[0]),
                  w_gate[0].astype(BF16), w_up[0].astype(BF16), w_down[0].astype(BF16))

    meta_x = jnp.zeros((1, META_PAD, D_MODEL), F32).at[0, :N_META].set(meta_tokens.astype(F32))
    cos_m, sin_m = _rope_tables(META_PAD, 0)
    p_meta, _, kt_meta, v_meta = _inproj(meta_x, cos_m, sin_m, g1, w_in_b, qg, kg, gsum, META_PAD)
    prev8 = p_meta[0, N_META - 8:N_META]
    zero8 = jnp.zeros_like(prev8)

    def group(x):
        bsz, length, _ = x.shape
        cos, sin = _rope_tables(length, N_META)
        p, q, kt, v = _inproj(x, cos, sin, g1, w_in_b, qg, kg, gsum, _tile_for(length, 256))
        pm = jnp.concatenate(
            [jnp.broadcast_to(p_meta[:, :N_META], (bsz, N_META, SHIFT_COLS)),
             p[:, :CHUNK - N_META]], axis=1)
        meta_prep = _prep(pm, zero8, prep_consts, CHUNK, N_META)
        zero_state = jnp.zeros((bsz, N_PAIRS, PAIR, PAIR), F32)
        s_meta = _scan(meta_prep, zero_state)[2]
        prep_out = _prep(p, prev8, prep_consts, _tile_for(length, 256), None)
        yf, yb, _ = _scan(prep_out, s_meta)
        od = _attention(q, kt, v, kt_meta, v_meta, lam_params, sg, lam_init,
                        _tile_for(length, 256), _tile_for(length, 512))
        return _out(x, yf, yb, prep_out[10], prep_out[9], od, out_consts, _tile_for(length, 256))

    return (group(x_prompt), group(x_sample))
```

```python
import functools
import math

import jax
import jax.numpy as jnp
from jax import lax
from jax.experimental import pallas as pl
from jax.experimental.pallas import tpu as pltpu

F32 = jnp.float32
BF16 = jnp.bfloat16

D_MODEL = 1024
N_META = 16
RWKV_WIDTH = 512
RWKV_HEAD = 64
DIFF_WIDTH = 512
DIFF_HEAD = 64
DIFF_HEADS = 4
N_MAPS = 2 * DIFF_HEADS
GATE_LORA = 128
LORA_PAIR = 128
SHIFT_COLS = 3 * RWKV_WIDTH + GATE_LORA + 2 * LORA_PAIR
IN_COLS = SHIFT_COLS + 3 * DIFF_WIDTH
D_FF = 2816
ROPE_THETA = 10000.0
RMS_EPS = 1e-6
GN_EPS = 64e-5
CHUNK = 64
SUB = 16
PAIR = 128
N_PAIRS = RWKV_WIDTH // PAIR
META_PAD = 128
VMEM_LIMIT_BYTES = 56 * 1024 * 1024
NEG_BIG = -1e30
HIGHEST = lax.Precision.HIGHEST


def _cparams(sem):
    return pltpu.CompilerParams(dimension_semantics=sem, vmem_limit_bytes=VMEM_LIMIT_BYTES)


def _const_spec(shape):
    nd = len(shape)
    return pl.BlockSpec(shape, lambda *_: (0,) * nd, pipeline_mode=pl.Buffered(1))


def _mm(a, b):
    return jnp.dot(a.astype(BF16), b.astype(BF16), preferred_element_type=F32)


def _mm_nt(a, b):
    return lax.dot_general(a.astype(BF16), b.astype(BF16), (((1,), (1,)), ((), ())),
                           preferred_element_type=F32)


def _split_dot(x, m, terms):
    acc = None
    rem = x
    for _ in range(terms):
        piece = rem.astype(BF16)
        part = jnp.dot(piece, m, preferred_element_type=F32)
        acc = part if acc is None else acc + part
        rem = rem - piece.astype(F32)
    return acc


def _tri_dot(tri, x):
    acc = None
    rem = x
    for _ in range(3):
        piece = rem.astype(BF16)
        part = jnp.dot(tri, piece, preferred_element_type=F32)
        acc = part if acc is None else acc + part
        rem = rem - piece.astype(F32)
    return acc


def _inproj_kernel(x_ref, g1_ref, w_ref, cos_ref, sin_ref, qg_ref, kg_ref, gs_ref,
                   p_ref, q_ref, kt_ref, v_ref):
    x = x_ref[0]
    ms = jnp.mean(x * x, axis=-1, keepdims=True)
    h = (x * lax.rsqrt(ms + RMS_EPS) * g1_ref[...]).astype(BF16)
    p_ref[0] = jnp.dot(h, w_ref[:, :SHIFT_COLS], preferred_element_type=F32)
    qkv = jnp.dot(h, w_ref[:, SHIFT_COLS:], preferred_element_type=F32)
    q = qkv[:, :DIFF_WIDTH]
    k = qkv[:, DIFF_WIDTH:2 * DIFF_WIDTH]
    v = qkv[:, 2 * DIFF_WIDTH:]

    reps = DIFF_WIDTH // cos_ref.shape[1]
    cos = jnp.tile(cos_ref[...], (1, reps))
    sin = jnp.tile(sin_ref[...], (1, reps))
    lane = lax.broadcasted_iota(jnp.int32, q.shape, 1)
    first_half = (lane % DIFF_HEAD) < (DIFF_HEAD // 2)
    gs = gs_ref[...]

    def norm_rope(t, g):
        ss = _split_dot(t * t, gs, 2)
        t = t * lax.rsqrt(ss * (1.0 / DIFF_HEAD) + RMS_EPS) * g
        rot = jnp.where(first_half,
                        pltpu.roll(t, DIFF_WIDTH - DIFF_HEAD // 2, 1),
                        pltpu.roll(t, DIFF_HEAD // 2, 1))
        return t * cos + rot * sin

    qr = norm_rope(q, qg_ref[...]) * (math.log2(math.e) / math.sqrt(DIFF_HEAD))
    kr = norm_rope(k, kg_ref[...])
    krt = kr.T
    for j in range(N_MAPS):
        q_ref[0, j] = qr[:, j * DIFF_HEAD:(j + 1) * DIFF_HEAD].astype(BF16)
        kt_ref[0, j] = krt[j * DIFF_HEAD:(j + 1) * DIFF_HEAD, :].astype(BF16)
    for hd in range(DIFF_HEADS):
        v_ref[0, hd] = v[:, hd * PAIR:(hd + 1) * PAIR].astype(BF16)


def _inproj(x, cos, sin, g1, w_in, qg, kg, gsum, tm):
    bsz, length, _ = x.shape
    grid = (bsz, length // tm)
    return pl.pallas_call(
        _inproj_kernel,
        grid=grid,
        in_specs=[
            pl.BlockSpec((1, tm, D_MODEL), lambda b, i: (b, i, 0)),
            _const_spec((1, D_MODEL)),
            _const_spec((D_MODEL, IN_COLS)),
            pl.BlockSpec((tm, PAIR), lambda b, i: (i, 0)),
            pl.BlockSpec((tm, PAIR), lambda b, i: (i, 0)),
            _const_spec((1, DIFF_WIDTH)),
            _const_spec((1, DIFF_WIDTH)),
            _const_spec((DIFF_WIDTH, DIFF_WIDTH)),
        ],
        out_specs=[
            pl.BlockSpec((1, tm, SHIFT_COLS), lambda b, i: (b, i, 0)),
            pl.BlockSpec((1, N_MAPS, tm, DIFF_HEAD), lambda b, i: (b, 0, i, 0)),
            pl.BlockSpec((1, N_MAPS, DIFF_HEAD, tm), lambda b, i: (b, 0, 0, i)),
            pl.BlockSpec((1, DIFF_HEADS, tm, PAIR), lambda b, i: (b, 0, i, 0)),
        ],
        out_shape=[
            jax.ShapeDtypeStruct((bsz, length, SHIFT_COLS), F32),
            jax.ShapeDtypeStruct((bsz, N_MAPS, length, DIFF_HEAD), BF16),
            jax.ShapeDtypeStruct((bsz, N_MAPS, DIFF_HEAD, length), BF16),
            jax.ShapeDtypeStruct((bsz, DIFF_HEADS, length, PAIR), BF16),
        ],
        compiler_params=_cparams(("parallel", "parallel")),
        name="inproj",
    )(x, g1, w_in, cos, sin, qg, kg, gsum)


def _prep_kernel(p_ref, ph_ref, nh_ref, p0_ref, mup_ref, mun_ref, w0_ref, w2_ref, a0_ref, a2_ref,
                 g2_ref, kk_ref, ka_ref, rk_ref, gs_ref,
                 raf_ref, kbf_ref, bktf_ref, wcf_ref, rab_ref, kbb_ref, bktb_ref, wcb_ref,
                 v_ref, g_ref, bonus_ref, *, valid_rows):
    i = pl.program_id(1)
    last = pl.num_programs(1) - 1
    p = p_ref[0]
    tm = p.shape[0]
    row = lax.broadcasted_iota(jnp.int32, p.shape, 0)
    prev_row = jnp.where(i == 0, p0_ref[7:8, :], ph_ref[0, 7:8, :])
    next_row = jnp.where(i == last, jnp.zeros_like(prev_row), nh_ref[0, 0:1, :])
    p_prev = jnp.where(row == 0, prev_row, pltpu.roll(p, 1, 0))
    p_next = jnp.where(row == tm - 1, next_row, pltpu.roll(p, tm - 1, 0))
    z = p + mup_ref[...] * (p_prev - p) + mun_ref[...] * (p_next - p)

    w = RWKV_WIDTH
    r = z[:, :w]
    k = z[:, w:2 * w]
    v = z[:, 2 * w:3 * w]
    gd = z[:, 3 * w:3 * w + GATE_LORA]
    wd = z[:, 3 * w + GATE_LORA:3 * w + GATE_LORA + LORA_PAIR]
    ad = z[:, 3 * w + GATE_LORA + LORA_PAIR:]

    wl = w0_ref[...] + jnp.dot(jnp.tanh(wd), w2_ref[...], preferred_element_type=F32, precision=HIGHEST)
    lw = (-math.exp(-0.5)) * jax.nn.sigmoid(wl)
    al = a0_ref[...] + jnp.dot(ad, a2_ref[...], preferred_element_type=F32, precision=HIGHEST)
    iclr = jax.nn.sigmoid(al)
    g = _mm(jax.nn.sigmoid(gd), g2_ref[...])

    gs = gs_ref[...]
    kk = k * kk_ref[...]
    kk = kk * lax.rsqrt(jnp.maximum(_split_dot(kk * kk, gs, 2), 1e-24))
    k_a = ka_ref[...]
    a_f = iclr[:, :w]
    a_b = iclr[:, w:]
    k_f = k * (1.0 + (a_f - 1.0) * k_a)
    k_b = k * (1.0 + (a_b - 1.0) * k_a)
    bonus = _split_dot(r * k_f * rk_ref[...], gs, 2) * v
    a_neg = -kk
    b_f = kk * a_f
    b_b = kk * a_b
    lw_f = lw[:, :w]
    lw_b = lw[:, w:]

    if valid_rows is not None:
        keep = lax.broadcasted_iota(jnp.int32, r.shape, 0) < valid_rows
        zero = jnp.zeros_like(r)
        r, v, a_neg = (jnp.where(keep, t, zero) for t in (r, v, a_neg))
        k_f, k_b, b_f, b_b = (jnp.where(keep, t, zero) for t in (k_f, k_b, b_f, b_b))
        lw_f, lw_b = (jnp.where(keep, t, zero) for t in (lw_f, lw_b))

    v_ref[0] = v.astype(BF16)
    g_ref[0] = g.astype(BF16)
    bonus_ref[0] = bonus.astype(BF16)

    ti = lax.broadcasted_iota(jnp.int32, (CHUNK, CHUNK), 0)
    si = lax.broadcasted_iota(jnp.int32, (CHUNK, CHUNK), 1)
    tri_f = jnp.where(si <= ti, 1.0, 0.0).astype(BF16)
    tri_b = jnp.where(si >= ti, 1.0, 0.0).astype(BF16)

    for c in range(tm // CHUNK):
        rows = slice(c * CHUNK, (c + 1) * CHUNK)
        r_c, v_c, a_c = r[rows], v[rows], a_neg[rows]
        for (lw_d, k_d, b_d, tri, fwd, ra_ref, kb_ref, bkt_ref, wc_ref) in (
                (lw_f, k_f, b_f, tri_f, True, raf_ref, kbf_ref, bktf_ref, wcf_ref),
                (lw_b, k_b, b_b, tri_b, False, rab_ref, kbb_ref, bktb_ref, wcb_ref)):
            lw_c = lw_d[rows]
            cum = _tri_dot(tri, lw_c)
            tot = cum[CHUNK - 1:CHUNK] if fwd else cum[0:1]
            e_excl = jnp.exp(cum - lw_c)
            e_read = jnp.exp(cum) if fwd else e_excl
            e_inv = jnp.exp(-cum)
            e_end = jnp.exp(tot - cum)
            ra_ref[0, 2 * c * CHUNK:(2 * c + 1) * CHUNK] = (r_c * e_read).astype(BF16)
            ra_ref[0, (2 * c + 1) * CHUNK:(2 * c + 2) * CHUNK] = (a_c * e_excl).astype(BF16)
            kb_ref[0, 2 * c * CHUNK:(2 * c + 1) * CHUNK] = (k_d[rows] * e_inv).astype(BF16)
            kb_ref[0, (2 * c + 1) * CHUNK:(2 * c + 2) * CHUNK] = (b_d[rows] * e_inv).astype(BF16)
            bk = jnp.concatenate([b_d[rows] * e_end, k_d[rows] * e_end], axis=0)
            bkt_ref[0, c * w:(c + 1) * w] = bk.T.astype(BF16)
            wc_ref[0, 8 * c:8 * (c + 1)] = jnp.broadcast_to(jnp.exp(tot), (8, w))


def _prep(p, prev8, consts, tm, valid_rows):
    bsz, length, _ = p.shape
    nt = length // tm
    nc = length // CHUNK
    cpt = tm // CHUNK
    hb = tm // 8
    n8 = length // 8
    mup, mun, w0, w2, a0, a2, g2, k_k, k_a, r_k, gsum = consts
    tok = lambda b, i: (b, i, 0)
    dir_specs = [
        pl.BlockSpec((1, 2 * tm, RWKV_WIDTH), tok),
        pl.BlockSpec((1, 2 * tm, RWKV_WIDTH), tok),
        pl.BlockSpec((1, cpt * RWKV_WIDTH, PAIR), tok),
        pl.BlockSpec((1, cpt * 8, RWKV_WIDTH), tok),
    ]
    dir_shapes = [
        jax.ShapeDtypeStruct((bsz, 2 * length, RWKV_WIDTH), BF16),
        jax.ShapeDtypeStruct((bsz, 2 * length, RWKV_WIDTH), BF16),
        jax.ShapeDtypeStruct((bsz, nc * RWKV_WIDTH, PAIR), BF16),
        jax.ShapeDtypeStruct((bsz, nc * 8, RWKV_WIDTH), F32),
    ]
    tok_spec = pl.BlockSpec((1, tm, RWKV_WIDTH), tok)
    tok_shape = jax.ShapeDtypeStruct((bsz, length, RWKV_WIDTH), BF16)
    return pl.pallas_call(
        functools.partial(_prep_kernel, valid_rows=valid_rows),
        grid=(bsz, nt),
        in_specs=[
            pl.BlockSpec((1, tm, SHIFT_COLS), tok),
            pl.BlockSpec((1, 8, SHIFT_COLS), lambda b, i: (b, jnp.maximum(i * hb - 1, 0), 0)),
            pl.BlockSpec((1, 8, SHIFT_COLS), lambda b, i: (b, jnp.minimum((i + 1) * hb, n8 - 1), 0)),
            _const_spec((8, SHIFT_COLS)),
            _const_spec((1, SHIFT_COLS)),
            _const_spec((1, SHIFT_COLS)),
            _const_spec((1, 2 * RWKV_WIDTH)),
            _const_spec((LORA_PAIR, 2 * RWKV_WIDTH)),
            _const_spec((1, 2 * RWKV_WIDTH)),
            _const_spec((LORA_PAIR, 2 * RWKV_WIDTH)),
            _const_spec((GATE_LORA, RWKV_WIDTH)),
            _const_spec((1, RWKV_WIDTH)),
            _const_spec((1, RWKV_WIDTH)),
            _const_spec((1, RWKV_WIDTH)),
            _const_spec((RWKV_WIDTH, RWKV_WIDTH)),
        ],
        out_specs=dir_specs + dir_specs + [tok_spec, tok_spec, tok_spec],
        out_shape=dir_shapes + dir_shapes + [tok_shape, tok_shape, tok_shape],
        compiler_params=_cparams(("parallel", "parallel")),
        name="rwkv_prep",
    )(p, p, p, prev8, mup, mun, w0, w2, a0, a2, g2, k_k, k_a, r_k, gsum)


def _row_blocks(x, lane_lo):
    zero = jnp.zeros_like(x)
    return jnp.concatenate([jnp.where(lane_lo, x, zero), jnp.where(lane_lo, zero, x)], axis=0)


def _scan_chains(chains):
    ti = lax.broadcasted_iota(jnp.int32, (CHUNK, PAIR), 0)
    li = lax.broadcasted_iota(jnp.int32, (CHUNK, PAIR), 1)
    si = li % CHUNK
    lane_lo = li < CHUNK
    lower, lower_eq, upper = ti > si, ti >= si, ti < si
    same_sub = (ti // SUB) == (si // SUB)
    eye_w = jnp.where(ti == si, 1.0, 0.0).astype(F32)
    rr = lax.broadcasted_iota(jnp.int32, (PAIR, PAIR), 0)
    cc = lax.broadcasted_iota(jnp.int32, (PAIR, PAIR), 1)
    same_head = (rr // RWKV_HEAD) == (cc // RWKV_HEAD)
    diag = rr == cc
    rb = lambda t: _row_blocks(t, lane_lo)
    each = lambda fn, *cols: [fn(*vals) for vals in zip(*cols)]

    sls = [slice(c[7] * PAIR, (c[7] + 1) * PAIR) for c in chains]
    mask_a = [lower if c[8] else upper for c in chains]
    mask_r = [lower_eq if c[8] else upper for c in chains]
    ra = [c[0][0, :, sl] for c, sl in zip(chains, sls)]
    kb = [c[1][0, :, sl] for c, sl in zip(chains, sls)]
    v2 = [c[4][0, :, sl] for c, sl in zip(chains, sls)]
    rbv = each(rb, v2)

    ak = each(lambda a, b: _mm_nt(a, rb(b[:CHUNK])), ra, kb)
    ab = each(lambda a, b: _mm_nt(a, rb(b[CHUNK:])), ra, kb)
    a_rk = each(lambda m, t: jnp.where(m, t[:CHUNK], 0.0), mask_r, ak)
    a_ak = each(lambda m, t: jnp.where(m, t[CHUNK:], 0.0), mask_a, ak)
    a_rb = each(lambda m, t: jnp.where(m, t[:CHUNK], 0.0), mask_r, ab)
    n_all = each(lambda m, t: jnp.where(m, t[CHUNK:], 0.0), mask_a, ab)

    d1 = each(lambda t: jnp.where(same_sub, t, 0.0), n_all)
    e1 = each(lambda t, d: t - d, n_all, d1)
    rd1 = each(rb, d1)
    d2 = each(_mm, d1, rd1)
    av = each(_mm, a_ak, rbv)
    rd2 = each(rb, d2)
    d4 = each(_mm, d2, rd2)
    t16 = each(lambda d: eye_w + d, d1)
    t16 = each(lambda t, r: t + _mm(t, r), t16, rd2)
    rd4 = each(rb, d4)
    d8 = each(_mm, d4, rd4)
    t16 = each(lambda t, r: t + _mm(t, r), t16, rd4)
    t16 = each(lambda t, d: t + _mm(t, rb(d)), t16, d8)
    m1 = each(lambda t, e: _mm(t, rb(e)), t16, e1)
    m2 = each(lambda m: _mm(m, rb(m)), m1)
    zz = each(lambda m: eye_w + m, m1)
    zz = each(lambda z, m: z + _mm(z, rb(m)), zz, m2)
    tinv = each(lambda z, t: _mm(z, rb(t)), zz, t16)

    tu = each(lambda t, r, a: _mm(t, jnp.concatenate([rb(r[CHUNK:]), rb(a.astype(BF16))], axis=1)), tinv, ra, av)
    ark_v = each(_mm, a_rk, rbv)
    ar = each(lambda a, t: _mm(a, jnp.concatenate([rb(t[:, :PAIR]), rb(t[:, PAIR:])], axis=1)), a_rb, tu)
    rp = each(lambda r, a: r[:CHUNK].astype(F32) + a[:, :PAIR], ra, ar)
    y0 = each(lambda a, w: a[:, PAIR:] + w, ar, ark_v)
    stack = each(lambda t, v: jnp.concatenate(
        [t, jnp.concatenate([jnp.zeros((CHUNK, PAIR), F32), v.astype(F32)], axis=1)], axis=0), tu, v2)
    pd = each(lambda c, sl, st: _mm(c[2][0, sl, :], st), chains, sls, stack)
    p2 = each(lambda c, sl, t: jnp.where(same_head, t[:, :PAIR], 0.0) + jnp.where(diag, c[3][0, 0:1, sl], 0.0),
              chains, sls, pd)
    s_old = [c[6][c[7]] for c in chains]
    y = each(lambda r, s, y_0: _mm(r, s) + y_0, rp, s_old, y0)
    s_new = each(lambda p, s, t: jnp.dot(p, s, preferred_element_type=F32, precision=HIGHEST)
                 + jnp.where(same_head, t[:, PAIR:], 0.0), p2, s_old, pd)
    for c, sl, y_c, s_c in zip(chains, sls, y, s_new):
        c[5][0, :, sl] = y_c
        c[6][c[7]] = s_c


def _scan_kernel(raf_ref, kbf_ref, bktf_ref, wcf_ref, vf_ref,
                 rab_ref, kbb_ref, bktb_ref, wcb_ref, vb_ref, s0_ref,
                 yf_ref, yb_ref, sout_ref, sf_ref, sb_ref):
    i = pl.program_id(1)

    @pl.when(i == 0)
    def _():
        sf_ref[...] = s0_ref[0]
        sb_ref[...] = jnp.zeros_like(sb_ref)

    chains = []
    for j in range(N_PAIRS):
        chains.append((raf_ref, kbf_ref, bktf_ref, wcf_ref, vf_ref, yf_ref, sf_ref, j, True))
        chains.append((rab_ref, kbb_ref, bktb_ref, wcb_ref, vb_ref, yb_ref, sb_ref, j, False))
    _scan_chains(chains)

    @pl.when(i == pl.num_programs(1) - 1)
    def _():
        sout_ref[0] = sf_ref[...]


def _scan(prep_out, s0):
    raf, kbf, bktf, wcf, rab, kbb, bktb, wcb, v = prep_out[:9]
    bsz, length, _ = v.shape
    nc = length // CHUNK
    fw = lambda b, i: (b, i, 0)
    bw = lambda b, i: (b, nc - 1 - i, 0)

    def dir_specs(im):
        return [
            pl.BlockSpec((1, 2 * CHUNK, RWKV_WIDTH), im),
            pl.BlockSpec((1, 2 * CHUNK, RWKV_WIDTH), im),
            pl.BlockSpec((1, RWKV_WIDTH, PAIR), im),
            pl.BlockSpec((1, 8, RWKV_WIDTH), im),
            pl.BlockSpec((1, CHUNK, RWKV_WIDTH), im),
        ]

    state_spec = pl.BlockSpec((1, N_PAIRS, PAIR, PAIR), lambda b, i: (b, 0, 0, 0))
    return pl.pallas_call(
        _scan_kernel,
        grid=(bsz, nc),
        in_specs=dir_specs(fw) + dir_specs(bw) + [state_spec],
        out_specs=[
            pl.BlockSpec((1, CHUNK, RWKV_WIDTH), fw),
            pl.BlockSpec((1, CHUNK, RWKV_WIDTH), bw),
            state_spec,
        ],
        out_shape=[
            jax.ShapeDtypeStruct((bsz, length, RWKV_WIDTH), F32),
            jax.ShapeDtypeStruct((bsz, length, RWKV_WIDTH), F32),
            jax.ShapeDtypeStruct((bsz, N_PAIRS, PAIR, PAIR), F32),
        ],
        scratch_shapes=[pltpu.VMEM((N_PAIRS, PAIR, PAIR), F32), pltpu.VMEM((N_PAIRS, PAIR, PAIR), F32)],
        compiler_params=_cparams(("parallel", "arbitrary")),
        name="rwkv_scan",
    )(raf, kbf, bktf, wcf, v, rab, kbb, bktb, wcb, v, s0)


def _attn_kernel(q_ref, kt_ref, v_ref, ktm_ref, vm_ref, lq1_ref, lk1_ref, lq2_ref, lk2_ref, sg_ref,
                 o_ref, m_ref, acc_ref, *, lam_init):
    kv = pl.program_id(2)
    tq = q_ref.shape[2]
    tk = kt_ref.shape[3]
    ones_tk = jnp.ones((tk, PAIR), BF16)
    ones_meta = jnp.ones((META_PAD, PAIR), BF16)

    @pl.when(kv == 0)
    def _():
        col = lax.broadcasted_iota(jnp.int32, (tq, META_PAD), 1)
        for hc in range(N_MAPS):
            s = jnp.dot(q_ref[0, hc], ktm_ref[0, hc], preferred_element_type=F32)
            s = jnp.where(col < N_META, s, NEG_BIG)
            m = jnp.max(s, axis=-1, keepdims=True)
            p = jnp.exp2(s - m).astype(BF16)
            vext = jnp.concatenate([vm_ref[0, hc // 2], ones_meta], axis=1)
            acc_ref[hc] = jnp.dot(p, vext, preferred_element_type=F32)
            m_ref[hc] = jnp.broadcast_to(m, (tq, PAIR))

    scores = [jnp.dot(q_ref[0, hc], kt_ref[0, hc], preferred_element_type=F32) for hc in range(N_MAPS)]
    for hc in range(N_MAPS):
        s = scores[hc]
        m_prev = m_ref[hc]
        m_next = jnp.maximum(m_prev, jnp.max(s, axis=-1, keepdims=True))
        alpha = jnp.exp2(m_prev - m_next)
        p = jnp.exp2(s - jnp.tile(m_next, (1, tk // PAIR))).astype(BF16)
        vext = jnp.concatenate([v_ref[0, hc // 2], ones_tk], axis=1)
        acc_ref[hc] = jnp.tile(alpha, (1, 2)) * acc_ref[hc] + jnp.dot(p, vext, preferred_element_type=F32)
        m_ref[hc] = m_next

    @pl.when(kv == pl.num_programs(2) - 1)
    def _():
        lam = (jnp.exp(jnp.sum(lq1_ref[...] * lk1_ref[...], axis=-1, keepdims=True))
               - jnp.exp(jnp.sum(lq2_ref[...] * lk2_ref[...], axis=-1, keepdims=True)) + lam_init)
        for hd in range(DIFF_HEADS):
            a1 = acc_ref[2 * hd]
            a2 = acc_ref[2 * hd + 1]
            o = a1[:, :PAIR] / a1[:, PAIR:] - lam * (a2[:, :PAIR] / a2[:, PAIR:])
            ms = jnp.mean(o * o, axis=-1, keepdims=True)
            o = o * lax.rsqrt(ms + RMS_EPS) * sg_ref[...] * (1.0 - lam_init)
            o_ref[0, :, hd * PAIR:(hd + 1) * PAIR] = o.astype(BF16)


def _attention(q, kt, v, ktm, vm, lam_params, subln_g, lam_init, tq, tk):
    bsz, _, length, _ = q.shape
    lq1, lk1, lq2, lk2 = lam_params
    return pl.pallas_call(
        functools.partial(_attn_kernel, lam_init=lam_init),
        grid=(bsz, length // tq, length // tk),
        in_specs=[
            pl.BlockSpec((1, N_MAPS, tq, DIFF_HEAD), lambda b, i, j: (b, 0, i, 0)),
            pl.BlockSpec((1, N_MAPS, DIFF_HEAD, tk), lambda b, i, j: (b, 0, 0, j)),
            pl.BlockSpec((1, DIFF_HEADS, tk, PAIR), lambda b, i, j: (b, 0, j, 0)),
            _const_spec((1, N_MAPS, DIFF_HEAD, META_PAD)),
            _const_spec((1, DIFF_HEADS, META_PAD, PAIR)),
            _const_spec((1, DIFF_HEAD)),
            _const_spec((1, DIFF_HEAD)),
            _const_spec((1, DIFF_HEAD)),
            _const_spec((1, DIFF_HEAD)),
            _const_spec((1, PAIR)),
        ],
        out_specs=pl.BlockSpec((1, tq, DIFF_WIDTH), lambda b, i, j: (b, i, 0)),
        out_shape=jax.ShapeDtypeStruct((bsz, length, DIFF_WIDTH), BF16),
        scratch_shapes=[pltpu.VMEM((N_MAPS, tq, PAIR), F32), pltpu.VMEM((N_MAPS, tq, 2 * PAIR), F32)],
        compiler_params=_cparams(("parallel", "parallel", "arbitrary")),
        name="diff_attn",
    )(q, kt, v, ktm, vm, lq1, lk1, lq2, lk2, subln_g)


def _out_kernel(x_ref, yf_ref, yb_ref, bonus_ref, g_ref, od_ref, lnw_ref, lnb_ref, gs_ref,
                wo_ref, g2n_ref, wg_ref, wu_ref, wd_ref, o_ref):
    gs = gs_ref[...]
    y = yf_ref[0] + yb_ref[0]
    mu = _split_dot(y, gs, 2) * (1.0 / RWKV_HEAD)
    yc = y - mu
    var = _split_dot(yc * yc, gs, 2) * (1.0 / RWKV_HEAD)
    yn = yc * lax.rsqrt(var + GN_EPS) * lnw_ref[...] + lnb_ref[...]
    o_rwkv = (yn + bonus_ref[0].astype(F32)) * g_ref[0].astype(F32)
    x1 = (x_ref[0]
          + jnp.dot(o_rwkv.astype(BF16), wo_ref[:RWKV_WIDTH, :], preferred_element_type=F32)
          + jnp.dot(od_ref[0], wo_ref[RWKV_WIDTH:, :], preferred_element_type=F32))
    ms = jnp.mean(x1 * x1, axis=-1, keepdims=True)
    h2 = (x1 * lax.rsqrt(ms + RMS_EPS) * g2n_ref[...]).astype(BF16)
    gate = jnp.dot(h2, wg_ref[...], preferred_element_type=F32)
    up = jnp.dot(h2, wu_ref[...], preferred_element_type=F32)
    act = (gate * jax.nn.sigmoid(gate) * up).astype(BF16)
    o_ref[0] = x1 + jnp.dot(act, wd_ref[...], preferred_element_type=F32)


def _out(x, yf, yb, bonus, g, od, consts, tm):
    bsz, length, _ = x.shape
    lnw, lnb, gsum, wo, g2n, wg, wu, wd = consts
    tok = lambda b, i: (b, i, 0)
    half = pl.BlockSpec((1, tm, RWKV_WIDTH), tok)
    return pl.pallas_call(
        _out_kernel,
        grid=(bsz, length // tm),
        in_specs=[
            pl.BlockSpec((1, tm, D_MODEL), tok), half, half, half, half, half,
            _const_spec((1, RWKV_WIDTH)),
            _const_spec((1, RWKV_WIDTH)),
            _const_spec((RWKV_WIDTH, RWKV_WIDTH)),
            _const_spec((D_MODEL, D_MODEL)),
            _const_spec((1, D_MODEL)),
            _const_spec((D_MODEL, D_FF)),
            _const_spec((D_MODEL, D_FF)),
            _const_spec((D_FF, D_MODEL)),
        ],
        out_specs=pl.BlockSpec((1, tm, D_MODEL), tok),
        out_shape=jax.ShapeDtypeStruct((bsz, length, D_MODEL), F32),
        compiler_params=_cparams(("parallel", "parallel")),
        name="out_ffn",
    )(x, yf, yb, bonus, g, od, lnw, lnb, gsum, wo, g2n, wg, wu, wd)


def _rope_tables(length, offset):
    pos = jnp.arange(length, dtype=F32) + offset
    inv = ROPE_THETA ** (-jnp.arange(0, DIFF_HEAD, 2, dtype=F32) / DIFF_HEAD)
    ang = pos[:, None] * inv[None, :]
    cos = jnp.concatenate([jnp.cos(ang)] * 4, axis=-1)
    sin = jnp.sin(ang)
    sin = jnp.concatenate([-sin, sin, -sin, sin], axis=-1)
    return cos, sin


def _block_diag2(a, b):
    za = jnp.zeros_like(a)
    return jnp.concatenate([jnp.concatenate([a, za], axis=1), jnp.concatenate([za, b], axis=1)], axis=0)


def _tile_for(length, pref):
    t = pref
    while length % t:
        t //= 2
    return t


def kernel(x_prompt, x_sample, meta_tokens, norm1_g, w_in, shift_mu_prev, shift_mu_next, w0_f, w2_f, w0_b, w2_b, a0_f, a2_f, a0_b, a2_b, g2, k_k, k_a, r_k, ln_x_w, ln_x_b, q_norm_g, k_norm_g, lam_q1, lam_k1, lam_q2, lam_k2, subln_g, w_out, norm2_g, w_gate, w_up, w_down):
    lam_init = 0.8 - 0.6 * math.exp(-0.3 * 0)
    row = lambda t: t.reshape(1, -1).astype(F32)
    gi = jnp.arange(RWKV_WIDTH) // RWKV_HEAD
    gsum = (gi[:, None] == gi[None, :]).astype(BF16)

    g1 = row(norm1_g[0])
    w_in_b = w_in[0].astype(BF16)
    qg = row(jnp.tile(q_norm_g[0], N_MAPS))
    kg = row(jnp.tile(k_norm_g[0], N_MAPS))
    prep_consts = (
        row(shift_mu_prev[0]), row(shift_mu_next[0]),
        row(jnp.concatenate([w0_f[0], w0_b[0]])), _block_diag2(w2_f[0], w2_b[0]).astype(F32),
        row(jnp.concatenate([a0_f[0], a0_b[0]])), _block_diag2(a2_f[0], a2_b[0]).astype(F32),
        g2[0].astype(BF16), row(k_k[0]), row(k_a[0]), row(r_k[0]), gsum,
    )
    lam_params = (row(lam_q1[0]), row(lam_k1[0]), row(lam_q2[0]), row(lam_k2[0]))
    sg = row(subln_g[0])
    out_consts = (row(ln_x_w[0]), row(ln_x_b[0]), gsum, w_out[0].astype(BF16), row(norm2_g[0]),
                  w_gate[0].astype(BF16), w_up[0].astype(BF16), w_down[0].astype(BF16))

    meta_x = jnp.zeros((1, META_PAD, D_MODEL), F32).at[0, :N_META].set(meta_tokens.astype(F32))
    cos_m, sin_m = _rope_tables(META_PAD, 0)
    p_meta, _, kt_meta, v_meta = _inproj(meta_x, cos_m, sin_m, g1, w_in_b, qg, kg, gsum, META_PAD)
    prev8 = p_meta[0, N_META - 8:N_META]
    zero8 = jnp.zeros_like(prev8)

    def group(x):
        bsz, length, _ = x.shape
        cos, sin = _rope_tables(length, N_META)
        p, q, kt, v = _inproj(x, cos, sin, g1, w_in_b, qg, kg, gsum, _tile_for(length, 256))
        pm = jnp.concatenate(
            [jnp.broadcast_to(p_meta[:, :N_META], (bsz, N_META, SHIFT_COLS)),
             p[:, :CHUNK - N_META]], axis=1)
        meta_prep = _prep(pm, zero8, prep_consts, CHUNK, N_META)
        zero_state = jnp.zeros((bsz, N_PAIRS, PAIR, PAIR), F32)
        s_meta = _scan(meta_prep, zero_state)[2]
        prep_out = _prep(p, prev8, prep_consts, _tile_for(length, 256), None)
        yf, yb, _ = _scan(prep_out, s_meta)
        od = _attention(q, kt, v, kt_meta, v_meta, lam_params, sg, lam_init,
                        _tile_for(length, 512), _tile_for(length, 512))
        return _out(x, yf, yb, prep_out[10], prep_out[9], od, out_consts, _tile_for(length, 256))

    return (group(x_prompt), group(x_sample))
```

```python
import functools
import math

import jax
import jax.numpy as jnp
from jax import lax
from jax.experimental import pallas as pl
from jax.experimental.pallas import tpu as pltpu

F32 = jnp.float32
BF16 = jnp.bfloat16

D_MODEL = 1024
N_META = 16
RWKV_WIDTH = 512
RWKV_HEAD = 64
DIFF_WIDTH = 512
DIFF_HEAD = 64
DIFF_HEADS = 4
N_MAPS = 2 * DIFF_HEADS
GATE_LORA = 128
LORA_PAIR = 128
SHIFT_COLS = 3 * RWKV_WIDTH + GATE_LORA + 2 * LORA_PAIR
IN_COLS = SHIFT_COLS + 3 * DIFF_WIDTH
D_FF = 2816
ROPE_THETA = 10000.0
RMS_EPS = 1e-6
GN_EPS = 64e-5
CHUNK = 64
SUB = 16
PAIR = 128
N_PAIRS = RWKV_WIDTH // PAIR
META_PAD = 128
VMEM_LIMIT_BYTES = 56 * 1024 * 1024
NEG_BIG = -1e30


def _cparams(sem):
    return pltpu.CompilerParams(dimension_semantics=sem, vmem_limit_bytes=VMEM_LIMIT_BYTES)


def _const_spec(shape):
    nd = len(shape)
    return pl.BlockSpec(shape, lambda *_: (0,) * nd, pipeline_mode=pl.Buffered(1))


def _mm(a, b):
    return jnp.dot(a.astype(BF16), b.astype(BF16), preferred_element_type=F32)


def _mm_nt(a, b):
    return lax.dot_general(a.astype(BF16), b.astype(BF16), (((1,), (1,)), ((), ())),
                           preferred_element_type=F32)


def _split_dot(x, m, terms):
    acc = None
    rem = x
    for _ in range(terms):
        piece = rem.astype(BF16)
        part = jnp.dot(piece, m, preferred_element_type=F32)
        acc = part if acc is None else acc + part
        rem = rem - piece.astype(F32)
    return acc


def _split2(x):
    hi = x.astype(BF16)
    return hi, (x - hi.astype(F32)).astype(BF16)


def _mm3(a_hi, a_lo, b_hi, b_lo):
    dot = functools.partial(jnp.dot, preferred_element_type=F32)
    return dot(a_hi, b_hi) + (dot(a_lo, b_hi) + dot(a_hi, b_lo))


def _tri_dot(tri, x):
    acc = None
    rem = x
    for _ in range(3):
        piece = rem.astype(BF16)
        part = jnp.dot(tri, piece, preferred_element_type=F32)
        acc = part if acc is None else acc + part
        rem = rem - piece.astype(F32)
    return acc


def _inproj_kernel(x_ref, g1_ref, w_ref, cos_ref, sin_ref, qg_ref, kg_ref, gs_ref,
                   p_ref, q_ref, kt_ref, v_ref):
    x = x_ref[0]
    ms = jnp.mean(x * x, axis=-1, keepdims=True)
    h = (x * lax.rsqrt(ms + RMS_EPS) * g1_ref[...]).astype(BF16)
    p_ref[0] = jnp.dot(h, w_ref[:, :SHIFT_COLS], preferred_element_type=F32)
    qkv = jnp.dot(h, w_ref[:, SHIFT_COLS:], preferred_element_type=F32)
    q = qkv[:, :DIFF_WIDTH]
    k = qkv[:, DIFF_WIDTH:2 * DIFF_WIDTH]
    v = qkv[:, 2 * DIFF_WIDTH:]

    reps = DIFF_WIDTH // cos_ref.shape[1]
    cos = jnp.tile(cos_ref[...], (1, reps))
    sin = jnp.tile(sin_ref[...], (1, reps))
    lane = lax.broadcasted_iota(jnp.int32, q.shape, 1)
    first_half = (lane % DIFF_HEAD) < (DIFF_HEAD // 2)
    gs = gs_ref[...]

    def norm_rope(t, g):
        ss = _split_dot(t * t, gs, 2)
        t = t * lax.rsqrt(ss * (1.0 / DIFF_HEAD) + RMS_EPS) * g
        rot = jnp.where(first_half,
                        pltpu.roll(t, DIFF_WIDTH - DIFF_HEAD // 2, 1),
                        pltpu.roll(t, DIFF_HEAD // 2, 1))
        return t * cos + rot * sin

    qr = norm_rope(q, qg_ref[...]) * (math.log2(math.e) / math.sqrt(DIFF_HEAD))
    kr = norm_rope(k, kg_ref[...])
    krt = kr.T
    for j in range(N_MAPS):
        q_ref[0, j] = qr[:, j * DIFF_HEAD:(j + 1) * DIFF_HEAD].astype(BF16)
        kt_ref[0, j] = krt[j * DIFF_HEAD:(j + 1) * DIFF_HEAD, :].astype(BF16)
    v_ref[0] = v.astype(BF16)


def _inproj(x, cos, sin, g1, w_in, qg, kg, gsum, tm):
    bsz, length, _ = x.shape
    grid = (bsz, length // tm)
    return pl.pallas_call(
        _inproj_kernel,
        grid=grid,
        in_specs=[
            pl.BlockSpec((1, tm, D_MODEL), lambda b, i: (b, i, 0)),
            _const_spec((1, D_MODEL)),
            _const_spec((D_MODEL, IN_COLS)),
            pl.BlockSpec((tm, PAIR), lambda b, i: (i, 0)),
            pl.BlockSpec((tm, PAIR), lambda b, i: (i, 0)),
            _const_spec((1, DIFF_WIDTH)),
            _const_spec((1, DIFF_WIDTH)),
            _const_spec((DIFF_WIDTH, DIFF_WIDTH)),
        ],
        out_specs=[
            pl.BlockSpec((1, tm, SHIFT_COLS), lambda b, i: (b, i, 0)),
            pl.BlockSpec((1, N_MAPS, tm, DIFF_HEAD), lambda b, i: (b, 0, i, 0)),
            pl.BlockSpec((1, N_MAPS, DIFF_HEAD, tm), lambda b, i: (b, 0, 0, i)),
            pl.BlockSpec((1, tm, DIFF_WIDTH), lambda b, i: (b, i, 0)),
        ],
        out_shape=[
            jax.ShapeDtypeStruct((bsz, length, SHIFT_COLS), F32),
            jax.ShapeDtypeStruct((bsz, N_MAPS, length, DIFF_HEAD), BF16),
            jax.ShapeDtypeStruct((bsz, N_MAPS, DIFF_HEAD, length), BF16),
            jax.ShapeDtypeStruct((bsz, length, DIFF_WIDTH), BF16),
        ],
        compiler_params=_cparams(("parallel", "parallel")),
        name="inproj",
    )(x, g1, w_in, cos, sin, qg, kg, gsum)


def _prep_kernel(p_ref, ph_ref, nh_ref, p0_ref, mup_ref, mun_ref, w0_ref, w2_ref, a0_ref, a2_ref,
                 g2_ref, kk_ref, ka_ref, rk_ref, gs_ref,
                 raf_ref, kbf_ref, bktf_ref, wcf_ref, rab_ref, kbb_ref, bktb_ref, wcb_ref,
                 v_ref, g_ref, bonus_ref, *, valid_rows):
    i = pl.program_id(1)
    last = pl.num_programs(1) - 1
    p = p_ref[0]
    tm = p.shape[0]
    row = lax.broadcasted_iota(jnp.int32, p.shape, 0)
    prev_row = jnp.where(i == 0, p0_ref[7:8, :], ph_ref[0, 7:8, :])
    next_row = jnp.where(i == last, jnp.zeros_like(prev_row), nh_ref[0, 0:1, :])
    p_prev = jnp.where(row == 0, prev_row, pltpu.roll(p, 1, 0))
    p_next = jnp.where(row == tm - 1, next_row, pltpu.roll(p, tm - 1, 0))
    z = p + mup_ref[...] * (p_prev - p) + mun_ref[...] * (p_next - p)

    w = RWKV_WIDTH
    r = z[:, :w]
    k = z[:, w:2 * w]
    v = z[:, 2 * w:3 * w]
    gd = z[:, 3 * w:3 * w + GATE_LORA]
    wd = z[:, 3 * w + GATE_LORA:3 * w + GATE_LORA + LORA_PAIR]
    ad = z[:, 3 * w + GATE_LORA + LORA_PAIR:]

    wl = w0_ref[...] + _mm3(*_split2(jnp.tanh(wd)), w2_ref[0], w2_ref[1])
    lw = (-math.exp(-0.5)) * jax.nn.sigmoid(wl)
    al = a0_ref[...] + _mm3(*_split2(ad), a2_ref[0], a2_ref[1])
    iclr = jax.nn.sigmoid(al)
    g = _mm(jax.nn.sigmoid(gd), g2_ref[...])

    gs = gs_ref[...]
    kk = k * kk_ref[...]
    kk = kk * lax.rsqrt(jnp.maximum(_split_dot(kk * kk, gs, 2), 1e-24))
    k_a = ka_ref[...]
    a_f = iclr[:, :w]
    a_b = iclr[:, w:]
    k_f = k * (1.0 + (a_f - 1.0) * k_a)
    k_b = k * (1.0 + (a_b - 1.0) * k_a)
    bonus = _split_dot(r * k_f * rk_ref[...], gs, 2) * v
    a_neg = -kk
    b_f = kk * a_f
    b_b = kk * a_b
    lw_f = lw[:, :w]
    lw_b = lw[:, w:]

    if valid_rows is not None:
        keep = lax.broadcasted_iota(jnp.int32, r.shape, 0) < valid_rows
        zero = jnp.zeros_like(r)
        r, v, a_neg = (jnp.where(keep, t, zero) for t in (r, v, a_neg))
        k_f, k_b, b_f, b_b = (jnp.where(keep, t, zero) for t in (k_f, k_b, b_f, b_b))
        lw_f, lw_b = (jnp.where(keep, t, zero) for t in (lw_f, lw_b))

    v_ref[0] = v.astype(BF16)
    g_ref[0] = g.astype(BF16)
    bonus_ref[0] = bonus.astype(BF16)

    ti = lax.broadcasted_iota(jnp.int32, (CHUNK, CHUNK), 0)
    si = lax.broadcasted_iota(jnp.int32, (CHUNK, CHUNK), 1)
    tri_f = jnp.where(si <= ti, 1.0, 0.0).astype(BF16)
    tri_b = jnp.where(si >= ti, 1.0, 0.0).astype(BF16)

    for c in range(tm // CHUNK):
        rows = slice(c * CHUNK, (c + 1) * CHUNK)
        r_c, v_c, a_c = r[rows], v[rows], a_neg[rows]
        for (lw_d, k_d, b_d, tri, fwd, ra_ref, kb_ref, bkt_ref, wc_ref) in (
                (lw_f, k_f, b_f, tri_f, True, raf_ref, kbf_ref, bktf_ref, wcf_ref),
                (lw_b, k_b, b_b, tri_b, False, rab_ref, kbb_ref, bktb_ref, wcb_ref)):
            lw_c = lw_d[rows]
            cum = _tri_dot(tri, lw_c)
            tot = cum[CHUNK - 1:CHUNK] if fwd else cum[0:1]
            e_excl = jnp.exp(cum - lw_c)
            e_read = jnp.exp(cum) if fwd else e_excl
            e_inv = jnp.exp(-cum)
            e_end = jnp.exp(tot - cum)
            ra_ref[0, 2 * c * CHUNK:(2 * c + 1) * CHUNK] = (r_c * e_read).astype(BF16)
            ra_ref[0, (2 * c + 1) * CHUNK:(2 * c + 2) * CHUNK] = (a_c * e_excl).astype(BF16)
            kb_ref[0, 2 * c * CHUNK:(2 * c + 1) * CHUNK] = (k_d[rows] * e_inv).astype(BF16)
            kb_ref[0, (2 * c + 1) * CHUNK:(2 * c + 2) * CHUNK] = (b_d[rows] * e_inv).astype(BF16)
            bk = jnp.concatenate([b_d[rows] * e_end, k_d[rows] * e_end], axis=0)
            bkt_ref[0, c * w:(c + 1) * w] = bk.T.astype(BF16)
            wc_ref[0, 8 * c:8 * (c + 1)] = jnp.broadcast_to(jnp.exp(tot), (8, w))


def _prep(p, prev8, consts, tm, valid_rows):
    bsz, length, _ = p.shape
    nt = length // tm
    nc = length // CHUNK
    cpt = tm // CHUNK
    hb = tm // 8
    n8 = length // 8
    mup, mun, w0, w2, a0, a2, g2, k_k, k_a, r_k, gsum = consts
    tok = lambda b, i: (b, i, 0)
    dir_specs = [
        pl.BlockSpec((1, 2 * tm, RWKV_WIDTH), tok),
        pl.BlockSpec((1, 2 * tm, RWKV_WIDTH), tok),
        pl.BlockSpec((1, cpt * RWKV_WIDTH, PAIR), tok),
        pl.BlockSpec((1, cpt * 8, RWKV_WIDTH), tok),
    ]
    dir_shapes = [
        jax.ShapeDtypeStruct((bsz, 2 * length, RWKV_WIDTH), BF16),
        jax.ShapeDtypeStruct((bsz, 2 * length, RWKV_WIDTH), BF16),
        jax.ShapeDtypeStruct((bsz, nc * RWKV_WIDTH, PAIR), BF16),
        jax.ShapeDtypeStruct((bsz, nc * 8, RWKV_WIDTH), F32),
    ]
    tok_spec = pl.BlockSpec((1, tm, RWKV_WIDTH), tok)
    tok_shape = jax.ShapeDtypeStruct((bsz, length, RWKV_WIDTH), BF16)
    return pl.pallas_call(
        functools.partial(_prep_kernel, valid_rows=valid_rows),
        grid=(bsz, nt),
        in_specs=[
            pl.BlockSpec((1, tm, SHIFT_COLS), tok),
            pl.BlockSpec((1, 8, SHIFT_COLS), lambda b, i: (b, jnp.maximum(i * hb - 1, 0), 0)),
            pl.BlockSpec((1, 8, SHIFT_COLS), lambda b, i: (b, jnp.minimum((i + 1) * hb, n8 - 1), 0)),
            _const_spec((8, SHIFT_COLS)),
            _const_spec((1, SHIFT_COLS)),
            _const_spec((1, SHIFT_COLS)),
            _const_spec((1, 2 * RWKV_WIDTH)),
            _const_spec((2, LORA_PAIR, 2 * RWKV_WIDTH)),
            _const_spec((1, 2 * RWKV_WIDTH)),
            _const_spec((2, LORA_PAIR, 2 * RWKV_WIDTH)),
            _const_spec((GATE_LORA, RWKV_WIDTH)),
            _const_spec((1, RWKV_WIDTH)),
            _const_spec((1, RWKV_WIDTH)),
            _const_spec((1, RWKV_WIDTH)),
            _const_spec((RWKV_WIDTH, RWKV_WIDTH)),
        ],
        out_specs=dir_specs + dir_specs + [tok_spec, tok_spec, tok_spec],
        out_shape=dir_shapes + dir_shapes + [tok_shape, tok_shape, tok_shape],
        compiler_params=_cparams(("parallel", "parallel")),
        name="rwkv_prep",
    )(p, p, p, prev8, mup, mun, w0, w2, a0, a2, g2, k_k, k_a, r_k, gsum)


def _row_blocks(x, lane_lo):
    zero = jnp.zeros_like(x)
    return jnp.concatenate([jnp.where(lane_lo, x, zero), jnp.where(lane_lo, zero, x)], axis=0)


def _scan_chains(dirs, cps):
    ti = lax.broadcasted_iota(jnp.int32, (CHUNK, PAIR), 0)
    li = lax.broadcasted_iota(jnp.int32, (CHUNK, PAIR), 1)
    si = li % CHUNK
    lane_lo = li < CHUNK
    lower, lower_eq, upper = ti > si, ti >= si, ti < si
    same_sub = (ti // SUB) == (si // SUB)
    eye_w = jnp.where(ti == si, 1.0, 0.0).astype(F32)
    rr = lax.broadcasted_iota(jnp.int32, (PAIR, PAIR), 0)
    cc = lax.broadcasted_iota(jnp.int32, (PAIR, PAIR), 1)
    same_head = (rr // RWKV_HEAD) == (cc // RWKV_HEAD)
    diag = rr == cc
    rb = lambda t: _row_blocks(t, lane_lo)
    each = lambda fn, *cols: [fn(*vals) for vals in zip(*cols)]

    chains = [(d, j, u) for u in range(cps) for d in dirs for j in range(N_PAIRS)]
    lanes = [slice(j * PAIR, (j + 1) * PAIR) for _, j, _ in chains]
    mask_a = [lower if d[7] else upper for d, _, _ in chains]
    mask_r = [lower_eq if d[7] else upper for d, _, _ in chains]
    ra = [d[0][0, 2 * u * CHUNK:2 * (u + 1) * CHUNK, ln] for (d, _, u), ln in zip(chains, lanes)]
    kb = [d[1][0, 2 * u * CHUNK:2 * (u + 1) * CHUNK, ln] for (d, _, u), ln in zip(chains, lanes)]
    v2 = [d[4][0, u * CHUNK:(u + 1) * CHUNK, ln] for (d, _, u), ln in zip(chains, lanes)]
    bkt = [d[2][0, u * RWKV_WIDTH + j * PAIR:u * RWKV_WIDTH + (j + 1) * PAIR, :] for d, j, u in chains]
    wc = [d[3][0, 8 * u:8 * u + 1, ln] for (d, _, u), ln in zip(chains, lanes)]
    rbv = each(rb, v2)

    ak = each(lambda a, b: _mm_nt(a, rb(b[:CHUNK])), ra, kb)
    ab = each(lambda a, b: _mm_nt(a, rb(b[CHUNK:])), ra, kb)
    a_rk = each(lambda m, t: jnp.where(m, t[:CHUNK], 0.0), mask_r, ak)
    a_ak = each(lambda m, t: jnp.where(m, t[CHUNK:], 0.0), mask_a, ak)
    a_rb = each(lambda m, t: jnp.where(m, t[:CHUNK], 0.0), mask_r, ab)
    n_all = each(lambda m, t: jnp.where(m, t[CHUNK:], 0.0), mask_a, ab)

    d1 = each(lambda t: jnp.where(same_sub, t, 0.0), n_all)
    e1 = each(lambda t, d: t - d, n_all, d1)
    rd1 = each(rb, d1)
    d2 = each(_mm, d1, rd1)
    av = each(_mm, a_ak, rbv)
    rd2 = each(rb, d2)
    d4 = each(_mm, d2, rd2)
    t16 = each(lambda d: eye_w + d, d1)
    t16 = each(lambda t, r: t + _mm(t, r), t16, rd2)
    rd4 = each(rb, d4)
    d8 = each(_mm, d4, rd4)
    t16 = each(lambda t, r: t + _mm(t, r), t16, rd4)
    t16 = each(lambda t, d: t + _mm(t, rb(d)), t16, d8)
    m1 = each(lambda t, e: _mm(t, rb(e)), t16, e1)
    m2 = each(lambda m: _mm(m, rb(m)), m1)
    zz = each(lambda m: eye_w + m, m1)
    zz = each(lambda z, m: z + _mm(z, rb(m)), zz, m2)
    tinv = each(lambda z, t: _mm(z, rb(t)), zz, t16)

    tu = each(lambda t, r, a: _mm(t, jnp.concatenate([rb(r[CHUNK:]), rb(a.astype(BF16))], axis=1)), tinv, ra, av)
    ark_v = each(_mm, a_rk, rbv)
    ar = each(lambda a, t: _mm(a, jnp.concatenate([rb(t[:, :PAIR]), rb(t[:, PAIR:])], axis=1)), a_rb, tu)
    rp = each(lambda r, a: (r[:CHUNK].astype(F32) + a[:, :PAIR]).astype(BF16), ra, ar)
    y0 = each(lambda a, w: a[:, PAIR:] + w, ar, ark_v)
    stack = each(lambda t, v: jnp.concatenate(
        [t, jnp.concatenate([jnp.zeros((CHUNK, PAIR), F32), v.astype(F32)], axis=1)], axis=0), tu, v2)
    pd = each(_mm, bkt, stack)
    p2 = each(lambda t, w: _split2(jnp.where(same_head, t[:, :PAIR], 0.0) + jnp.where(diag, w, 0.0)), pd, wc)
    d02 = each(lambda t: jnp.where(same_head, t[:, PAIR:], 0.0), pd)

    state = {(id(d), j): d[6][j] for d in dirs for j in range(N_PAIRS)}
    for step in range(cps):
        for di, d in enumerate(dirs):
            u = step if d[7] else cps - 1 - step
            for j in range(N_PAIRS):
                c = (u * len(dirs) + di) * N_PAIRS + j
                s_hi, s_lo = _split2(state[(id(d), j)])
                d[5][0, u * CHUNK:(u + 1) * CHUNK, lanes[c]] = (
                    jnp.dot(rp[c], s_hi, preferred_element_type=F32) + y0[c])
                state[(id(d), j)] = _mm3(p2[c][0], p2[c][1], s_hi, s_lo) + d02[c]
    for d in dirs:
        for j in range(N_PAIRS):
            d[6][j] = state[(id(d), j)]


def _scan_kernel(raf_ref, kbf_ref, bktf_ref, wcf_ref, vf_ref,
                 rab_ref, kbb_ref, bktb_ref, wcb_ref, vb_ref, s0_ref,
                 yf_ref, yb_ref, sout_ref, sf_ref, sb_ref, *, cps):
    i = pl.program_id(1)

    @pl.when(i == 0)
    def _():
        sf_ref[...] = s0_ref[0]
        sb_ref[...] = jnp.zeros_like(sb_ref)

    _scan_chains(((raf_ref, kbf_ref, bktf_ref, wcf_ref, vf_ref, yf_ref, sf_ref, True),
                  (rab_ref, kbb_ref, bktb_ref, wcb_ref, vb_ref, yb_ref, sb_ref, False)), cps)

    @pl.when(i == pl.num_programs(1) - 1)
    def _():
        sout_ref[0] = sf_ref[...]


def _scan(prep_out, s0, cps):
    raf, kbf, bktf, wcf, rab, kbb, bktb, wcb, v = prep_out[:9]
    bsz, length, _ = v.shape
    steps = length // (cps * CHUNK)
    fw = lambda b, i: (b, i, 0)
    bw = lambda b, i: (b, steps - 1 - i, 0)

    def dir_specs(im):
        return [
            pl.BlockSpec((1, 2 * cps * CHUNK, RWKV_WIDTH), im),
            pl.BlockSpec((1, 2 * cps * CHUNK, RWKV_WIDTH), im),
            pl.BlockSpec((1, cps * RWKV_WIDTH, PAIR), im),
            pl.BlockSpec((1, cps * 8, RWKV_WIDTH), im),
            pl.BlockSpec((1, cps * CHUNK, RWKV_WIDTH), im),
        ]

    state_spec = pl.BlockSpec((1, N_PAIRS, PAIR, PAIR), lambda b, i: (b, 0, 0, 0))
    return pl.pallas_call(
        functools.partial(_scan_kernel, cps=cps),
        grid=(bsz, steps),
        in_specs=dir_specs(fw) + dir_specs(bw) + [state_spec],
        out_specs=[
            pl.BlockSpec((1, cps * CHUNK, RWKV_WIDTH), fw),
            pl.BlockSpec((1, cps * CHUNK, RWKV_WIDTH), bw),
            state_spec,
        ],
        out_shape=[
            jax.ShapeDtypeStruct((bsz, length, RWKV_WIDTH), F32),
            jax.ShapeDtypeStruct((bsz, length, RWKV_WIDTH), F32),
            jax.ShapeDtypeStruct((bsz, N_PAIRS, PAIR, PAIR), F32),
        ],
        scratch_shapes=[pltpu.VMEM((N_PAIRS, PAIR, PAIR), F32), pltpu.VMEM((N_PAIRS, PAIR, PAIR), F32)],
        compiler_params=_cparams(("parallel", "arbitrary")),
        name="rwkv_scan",
    )(raf, kbf, bktf, wcf, v, rab, kbb, bktb, wcb, v, s0)


def _attn_kernel(q_ref, kt_ref, v_ref, ktm_ref, vm_ref, lq1_ref, lk1_ref, lq2_ref, lk2_ref, sg_ref,
                 o_ref, m_ref, acc_ref, *, lam_init):
    kv = pl.program_id(2)
    tq = q_ref.shape[2]
    tk = kt_ref.shape[3]
    ones_tk = jnp.ones((tk, PAIR), BF16)
    ones_meta = jnp.ones((META_PAD, PAIR), BF16)

    @pl.when(kv == 0)
    def _():
        col = lax.broadcasted_iota(jnp.int32, (tq, META_PAD), 1)
        for hc in range(N_MAPS):
            s = jnp.dot(q_ref[0, hc], ktm_ref[0, hc], preferred_element_type=F32)
            s = jnp.where(col < N_META, s, NEG_BIG)
            m = jnp.max(s, axis=-1, keepdims=True)
            p = jnp.exp2(s - m).astype(BF16)
            vext = jnp.concatenate([vm_ref[0, :, (hc // 2) * PAIR:(hc // 2 + 1) * PAIR], ones_meta], axis=1)
            acc_ref[hc] = jnp.dot(p, vext, preferred_element_type=F32)
            m_ref[hc] = jnp.broadcast_to(m, (tq, PAIR))

    scores = [jnp.dot(q_ref[0, hc], kt_ref[0, hc], preferred_element_type=F32) for hc in range(N_MAPS)]
    for hc in range(N_MAPS):
        s = scores[hc]
        m_prev = m_ref[hc]
        m_next = jnp.maximum(m_prev, jnp.max(s, axis=-1, keepdims=True))
        alpha = jnp.exp2(m_prev - m_next)
        p = jnp.exp2(s - jnp.tile(m_next, (1, tk // PAIR))).astype(BF16)
        vext = jnp.concatenate([v_ref[0, :, (hc // 2) * PAIR:(hc // 2 + 1) * PAIR], ones_tk], axis=1)
        acc_ref[hc] = jnp.tile(alpha, (1, 2)) * acc_ref[hc] + jnp.dot(p, vext, preferred_element_type=F32)
        m_ref[hc] = m_next

    @pl.when(kv == pl.num_programs(2) - 1)
    def _():
        lam = (jnp.exp(jnp.sum(lq1_ref[...] * lk1_ref[...], axis=-1, keepdims=True))
               - jnp.exp(jnp.sum(lq2_ref[...] * lk2_ref[...], axis=-1, keepdims=True)) + lam_init)
        for hd in range(DIFF_HEADS):
            a1 = acc_ref[2 * hd]
            a2 = acc_ref[2 * hd + 1]
            o = a1[:, :PAIR] / a1[:, PAIR:] - lam * (a2[:, :PAIR] / a2[:, PAIR:])
            ms = jnp.mean(o * o, axis=-1, keepdims=True)
            o = o * lax.rsqrt(ms + RMS_EPS) * sg_ref[...] * (1.0 - lam_init)
            o_ref[0, :, hd * PAIR:(hd + 1) * PAIR] = o.astype(BF16)


def _attention(q, kt, v, ktm, vm, lam_params, subln_g, lam_init, tq, tk):
    bsz, _, length, _ = q.shape
    lq1, lk1, lq2, lk2 = lam_params
    return pl.pallas_call(
        functools.partial(_attn_kernel, lam_init=lam_init),
        grid=(bsz, length // tq, length // tk),
        in_specs=[
            pl.BlockSpec((1, N_MAPS, tq, DIFF_HEAD), lambda b, i, j: (b, 0, i, 0)),
            pl.BlockSpec((1, N_MAPS, DIFF_HEAD, tk), lambda b, i, j: (b, 0, 0, j)),
            pl.BlockSpec((1, tk, DIFF_WIDTH), lambda b, i, j: (b, j, 0)),
            _const_spec((1, N_MAPS, DIFF_HEAD, META_PAD)),
            _const_spec((1, META_PAD, DIFF_WIDTH)),
            _const_spec((1, DIFF_HEAD)),
            _const_spec((1, DIFF_HEAD)),
            _const_spec((1, DIFF_HEAD)),
            _const_spec((1, DIFF_HEAD)),
            _const_spec((1, PAIR)),
        ],
        out_specs=pl.BlockSpec((1, tq, DIFF_WIDTH), lambda b, i, j: (b, i, 0)),
        out_shape=jax.ShapeDtypeStruct((bsz, length, DIFF_WIDTH), BF16),
        scratch_shapes=[pltpu.VMEM((N_MAPS, tq, PAIR), F32), pltpu.VMEM((N_MAPS, tq, 2 * PAIR), F32)],
        compiler_params=_cparams(("parallel", "parallel", "arbitrary")),
        name="diff_attn",
    )(q, kt, v, ktm, vm, lq1, lk1, lq2, lk2, subln_g)


def _out_kernel(x_ref, yf_ref, yb_ref, bonus_ref, g_ref, od_ref, lnw_ref, lnb_ref, gs_ref,
                wo_ref, g2n_ref, wg_ref, wu_ref, wd_ref, o_ref):
    gs = gs_ref[...]
    y = yf_ref[0] + yb_ref[0]
    mu = _split_dot(y, gs, 2) * (1.0 / RWKV_HEAD)
    yc = y - mu
    var = _split_dot(yc * yc, gs, 2) * (1.0 / RWKV_HEAD)
    yn = yc * lax.rsqrt(var + GN_EPS) * lnw_ref[...] + lnb_ref[...]
    o_rwkv = (yn + bonus_ref[0].astype(F32)) * g_ref[0].astype(F32)
    x1 = (x_ref[0]
          + jnp.dot(o_rwkv.astype(BF16), wo_ref[:RWKV_WIDTH, :], preferred_element_type=F32)
          + jnp.dot(od_ref[0], wo_ref[RWKV_WIDTH:, :], preferred_element_type=F32))
    ms = jnp.mean(x1 * x1, axis=-1, keepdims=True)
    h2 = (x1 * lax.rsqrt(ms + RMS_EPS) * g2n_ref[...]).astype(BF16)
    gate = jnp.dot(h2, wg_ref[...], preferred_element_type=F32)
    up = jnp.dot(h2, wu_ref[...], preferred_element_type=F32)
    act = (gate * jax.nn.sigmoid(gate) * up).astype(BF16)
    o_ref[0] = x1 + jnp.dot(act, wd_ref[...], preferred_element_type=F32)


def _out(x, yf, yb, bonus, g, od, consts, tm):
    bsz, length, _ = x.shape
    lnw, lnb, gsum, wo, g2n, wg, wu, wd = consts
    tok = lambda b, i: (b, i, 0)
    half = pl.BlockSpec((1, tm, RWKV_WIDTH), tok)
    return pl.pallas_call(
        _out_kernel,
        grid=(bsz, length // tm),
        in_specs=[
            pl.BlockSpec((1, tm, D_MODEL), tok), half, half, half, half, half,
            _const_spec((1, RWKV_WIDTH)),
            _const_spec((1, RWKV_WIDTH)),
            _const_spec((RWKV_WIDTH, RWKV_WIDTH)),
            _const_spec((D_MODEL, D_MODEL)),
            _const_spec((1, D_MODEL)),
            _const_spec((D_MODEL, D_FF)),
            _const_spec((D_MODEL, D_FF)),
            _const_spec((D_FF, D_MODEL)),
        ],
        out_specs=pl.BlockSpec((1, tm, D_MODEL), tok),
        out_shape=jax.ShapeDtypeStruct((bsz, length, D_MODEL), F32),
        compiler_params=_cparams(("parallel", "parallel")),
        name="out_ffn",
    )(x, yf, yb, bonus, g, od, lnw, lnb, gsum, wo, g2n, wg, wu, wd)


def _rope_tables(length, offset):
    pos = jnp.arange(length, dtype=F32) + offset
    inv = ROPE_THETA ** (-jnp.arange(0, DIFF_HEAD, 2, dtype=F32) / DIFF_HEAD)
    ang = pos[:, None] * inv[None, :]
    cos = jnp.concatenate([jnp.cos(ang)] * 4, axis=-1)
    sin = jnp.sin(ang)
    sin = jnp.concatenate([-sin, sin, -sin, sin], axis=-1)
    return cos, sin


def _block_diag2(a, b):
    za = jnp.zeros_like(a)
    return jnp.concatenate([jnp.concatenate([a, za], axis=1), jnp.concatenate([za, b], axis=1)], axis=0)


def _tile_for(length, pref):
    t = pref
    while length % t:
        t //= 2
    return t


def kernel(x_prompt, x_sample, meta_tokens, norm1_g, w_in, shift_mu_prev, shift_mu_next, w0_f, w2_f, w0_b, w2_b, a0_f, a2_f, a0_b, a2_b, g2, k_k, k_a, r_k, ln_x_w, ln_x_b, q_norm_g, k_norm_g, lam_q1, lam_k1, lam_q2, lam_k2, subln_g, w_out, norm2_g, w_gate, w_up, w_down):
    lam_init = 0.8 - 0.6 * math.exp(-0.3 * 0)
    row = lambda t: t.reshape(1, -1).astype(F32)
    gi = jnp.arange(RWKV_WIDTH) // RWKV_HEAD
    gsum = (gi[:, None] == gi[None, :]).astype(BF16)

    g1 = row(norm1_g[0])
    w_in_b = w_in[0].astype(BF16)
    qg = row(jnp.tile(q_norm_g[0], N_MAPS))
    kg = row(jnp.tile(k_norm_g[0], N_MAPS))
    prep_consts = (
        row(shift_mu_prev[0]), row(shift_mu_next[0]),
        row(jnp.concatenate([w0_f[0], w0_b[0]])), jnp.stack(_split2(_block_diag2(w2_f[0], w2_b[0]).astype(F32))),
        row(jnp.concatenate([a0_f[0], a0_b[0]])), jnp.stack(_split2(_block_diag2(a2_f[0], a2_b[0]).astype(F32))),
        g2[0].astype(BF16), row(k_k[0]), row(k_a[0]), row(r_k[0]), gsum,
    )
    lam_params = (row(lam_q1[0]), row(lam_k1[0]), row(lam_q2[0]), row(lam_k2[0]))
    sg = row(subln_g[0])
    out_consts = (row(ln_x_w[0]), row(ln_x_b[0]), gsum, w_out[0].astype(BF16), row(norm2_g[0]),
                  w_gate[0].astype(BF16), w_up[0].astype(BF16), w_down[0].astype(BF16))

    meta_x = jnp.zeros((1, META_PAD, D_MODEL), F32).at[0, :N_META].set(meta_tokens.astype(F32))
    cos_m, sin_m = _rope_tables(META_PAD, 0)
    p_meta, _, kt_meta, v_meta = _inproj(meta_x, cos_m, sin_m, g1, w_in_b, qg, kg, gsum, META_PAD)
    prev8 = p_meta[0, N_META - 8:N_META]
    zero8 = jnp.zeros_like(prev8)

    def group(x):
        bsz, length, _ = x.shape
        cos, sin = _rope_tables(length, N_META)
        p, q, kt, v = _inproj(x, cos, sin, g1, w_in_b, qg, kg, gsum, _tile_for(length, 512))
        pm = jnp.concatenate(
            [jnp.broadcast_to(p_meta[:, :N_META], (bsz, N_META, SHIFT_COLS)),
             p[:, :CHUNK - N_META]], axis=1)
        meta_prep = _prep(pm, zero8, prep_consts, CHUNK, N_META)
        zero_state = jnp.zeros((bsz, N_PAIRS, PAIR, PAIR), F32)
        s_meta = _scan(meta_prep, zero_state, 1)[2]
        prep_out = _prep(p, prev8, prep_consts, _tile_for(length, 256), None)
        yf, yb, _ = _scan(prep_out, s_meta, 2)
        od = _attention(q, kt, v, kt_meta, v_meta, lam_params, sg, lam_init,
                        _tile_for(length, 512), _tile_for(length, 1024))
        return _out(x, yf, yb, prep_out[10], prep_out[9], od, out_consts, _tile_for(length, 512))

    return (group(x_prompt), group(x_sample))
```

```python
import functools
import math

import jax
import jax.numpy as jnp
from jax import lax
from jax.experimental import pallas as pl
from jax.experimental.pallas import tpu as pltpu

F32 = jnp.float32
BF16 = jnp.bfloat16

D_MODEL = 1024
N_META = 16
RWKV_WIDTH = 512
RWKV_HEAD = 64
DIFF_WIDTH = 512
DIFF_HEAD = 64
DIFF_HEADS = 4
N_MAPS = 2 * DIFF_HEADS
GATE_LORA = 128
LORA_PAIR = 128
SHIFT_COLS = 3 * RWKV_WIDTH + GATE_LORA + 2 * LORA_PAIR
IN_COLS = SHIFT_COLS + 3 * DIFF_WIDTH
D_FF = 2816
ROPE_THETA = 10000.0
RMS_EPS = 1e-6
GN_EPS = 64e-5
CHUNK = 64
SUB = 16
PAIR = 128
N_PAIRS = RWKV_WIDTH // PAIR
META_PAD = 128
VMEM_LIMIT_BYTES = 56 * 1024 * 1024
NEG_BIG = -1e30


def _cparams(sem):
    return pltpu.CompilerParams(dimension_semantics=sem, vmem_limit_bytes=VMEM_LIMIT_BYTES)


def _const_spec(shape):
    nd = len(shape)
    return pl.BlockSpec(shape, lambda *_: (0,) * nd, pipeline_mode=pl.Buffered(1))


def _mm(a, b):
    return jnp.dot(a.astype(BF16), b.astype(BF16), preferred_element_type=F32)


def _mm_nt(a, b):
    return lax.dot_general(a.astype(BF16), b.astype(BF16), (((1,), (1,)), ((), ())),
                           preferred_element_type=F32)


def _split_dot(x, m, terms):
    outs = []
    for c in range(x.shape[1] // PAIR):
        acc = None
        rem = x[:, c * PAIR:(c + 1) * PAIR]
        for t in range(terms):
            piece = rem.astype(BF16)
            part = jnp.dot(piece, m, preferred_element_type=F32)
            acc = part if acc is None else acc + part
            if t + 1 < terms:
                rem = rem - piece.astype(F32)
        outs.append(acc)
    return jnp.concatenate(outs, axis=1)


def _split2(x):
    hi = x.astype(BF16)
    return hi, (x - hi.astype(F32)).astype(BF16)


def _mm3(a_hi, a_lo, b_hi, b_lo):
    dot = functools.partial(jnp.dot, preferred_element_type=F32)
    return dot(a_hi, b_hi) + (dot(a_lo, b_hi) + dot(a_hi, b_lo))


def _tri_dot(tri, x):
    acc = None
    rem = x
    for _ in range(3):
        piece = rem.astype(BF16)
        part = jnp.dot(tri, piece, preferred_element_type=F32)
        acc = part if acc is None else acc + part
        rem = rem - piece.astype(F32)
    return acc


def _inproj_kernel(x_ref, g1_ref, w_ref, cos_ref, sin_ref, qg_ref, kg_ref, gs_ref,
                   p_ref, q_ref, kt_ref, v_ref):
    x = x_ref[0]
    ms = jnp.mean(x * x, axis=-1, keepdims=True)
    h = (x * lax.rsqrt(ms + RMS_EPS) * g1_ref[...]).astype(BF16)
    p_ref[0] = jnp.dot(h, w_ref[:, :SHIFT_COLS], preferred_element_type=F32)
    qkv = jnp.dot(h, w_ref[:, SHIFT_COLS:], preferred_element_type=F32)
    q = qkv[:, :DIFF_WIDTH]
    k = qkv[:, DIFF_WIDTH:2 * DIFF_WIDTH]
    v = qkv[:, 2 * DIFF_WIDTH:]

    reps = DIFF_WIDTH // cos_ref.shape[1]
    cos = jnp.tile(cos_ref[...], (1, reps))
    sin = jnp.tile(sin_ref[...], (1, reps))
    lane = lax.broadcasted_iota(jnp.int32, q.shape, 1)
    first_half = (lane % DIFF_HEAD) < (DIFF_HEAD // 2)
    gs = gs_ref[...]

    def norm_rope(t, g):
        ss = _split_dot(t * t, gs, 1)
        t = t * lax.rsqrt(ss * (1.0 / DIFF_HEAD) + RMS_EPS) * g
        rot = jnp.where(first_half,
                        pltpu.roll(t, DIFF_WIDTH - DIFF_HEAD // 2, 1),
                        pltpu.roll(t, DIFF_HEAD // 2, 1))
        return t * cos + rot * sin

    qr = norm_rope(q, qg_ref[...]) * (math.log2(math.e) / math.sqrt(DIFF_HEAD))
    kr = norm_rope(k, kg_ref[...])
    krt = kr.T
    for j in range(N_MAPS):
        q_ref[0, j] = qr[:, j * DIFF_HEAD:(j + 1) * DIFF_HEAD].astype(BF16)
        kt_ref[0, j] = krt[j * DIFF_HEAD:(j + 1) * DIFF_HEAD, :].astype(BF16)
    v_ref[0] = v.astype(BF16)


def _inproj(x, cos, sin, g1, w_in, qg, kg, gsum, tm):
    bsz, length, _ = x.shape
    grid = (bsz, length // tm)
    return pl.pallas_call(
        _inproj_kernel,
        grid=grid,
        in_specs=[
            pl.BlockSpec((1, tm, D_MODEL), lambda b, i: (b, i, 0)),
            _const_spec((1, D_MODEL)),
            _const_spec((D_MODEL, IN_COLS)),
            pl.BlockSpec((tm, PAIR), lambda b, i: (i, 0)),
            pl.BlockSpec((tm, PAIR), lambda b, i: (i, 0)),
            _const_spec((1, DIFF_WIDTH)),
            _const_spec((1, DIFF_WIDTH)),
            _const_spec((PAIR, PAIR)),
        ],
        out_specs=[
            pl.BlockSpec((1, tm, SHIFT_COLS), lambda b, i: (b, i, 0)),
            pl.BlockSpec((1, N_MAPS, tm, DIFF_HEAD), lambda b, i: (b, 0, i, 0)),
            pl.BlockSpec((1, N_MAPS, DIFF_HEAD, tm), lambda b, i: (b, 0, 0, i)),
            pl.BlockSpec((1, tm, DIFF_WIDTH), lambda b, i: (b, i, 0)),
        ],
        out_shape=[
            jax.ShapeDtypeStruct((bsz, length, SHIFT_COLS), F32),
            jax.ShapeDtypeStruct((bsz, N_MAPS, length, DIFF_HEAD), BF16),
            jax.ShapeDtypeStruct((bsz, N_MAPS, DIFF_HEAD, length), BF16),
            jax.ShapeDtypeStruct((bsz, length, DIFF_WIDTH), BF16),
        ],
        compiler_params=_cparams(("parallel", "parallel")),
        name="inproj",
    )(x, g1, w_in, cos, sin, qg, kg, gsum)


def _prep_kernel(p_ref, ph_ref, nh_ref, p0_ref, mup_ref, mun_ref, w0_ref, w2_ref, a0_ref, a2_ref,
                 g2_ref, kk_ref, ka_ref, rk_ref, gs_ref,
                 raf_ref, kbf_ref, bktf_ref, wcf_ref, rab_ref, kbb_ref, bktb_ref, wcb_ref,
                 v_ref, g_ref, bonus_ref, *, valid_rows):
    i = pl.program_id(1)
    last = pl.num_programs(1) - 1
    p = p_ref[0]
    tm = p.shape[0]
    row = lax.broadcasted_iota(jnp.int32, p.shape, 0)
    prev_row = jnp.where(i == 0, p0_ref[7:8, :], ph_ref[0, 7:8, :])
    next_row = jnp.where(i == last, jnp.zeros_like(prev_row), nh_ref[0, 0:1, :])
    p_prev = jnp.where(row == 0, prev_row, pltpu.roll(p, 1, 0))
    p_next = jnp.where(row == tm - 1, next_row, pltpu.roll(p, tm - 1, 0))
    z = p + mup_ref[...] * (p_prev - p) + mun_ref[...] * (p_next - p)

    w = RWKV_WIDTH
    r = z[:, :w]
    k = z[:, w:2 * w]
    v = z[:, 2 * w:3 * w]
    gd = z[:, 3 * w:3 * w + GATE_LORA]
    wd = z[:, 3 * w + GATE_LORA:3 * w + GATE_LORA + LORA_PAIR]
    ad = z[:, 3 * w + GATE_LORA + LORA_PAIR:]

    wl = w0_ref[...] + _mm3(*_split2(jnp.tanh(wd)), w2_ref[0], w2_ref[1])
    lw = (-math.exp(-0.5) * math.log2(math.e)) * jax.nn.sigmoid(wl)
    al = a0_ref[...] + _mm3(*_split2(ad), a2_ref[0], a2_ref[1])
    iclr = jax.nn.sigmoid(al)
    g = _mm(jax.nn.sigmoid(gd), g2_ref[...])

    gs = gs_ref[...]
    kk = k * kk_ref[...]
    kk = kk * lax.rsqrt(jnp.maximum(_split_dot(kk * kk, gs, 2), 1e-24))
    k_a = ka_ref[...]
    a_f = iclr[:, :w]
    a_b = iclr[:, w:]
    k_f = k * (1.0 + (a_f - 1.0) * k_a)
    k_b = k * (1.0 + (a_b - 1.0) * k_a)
    bonus = _split_dot(r * k_f * rk_ref[...], gs, 2) * v
    a_neg = -kk
    b_f = kk * a_f
    b_b = kk * a_b
    lw_f = lw[:, :w]
    lw_b = lw[:, w:]

    if valid_rows is not None:
        keep = lax.broadcasted_iota(jnp.int32, r.shape, 0) < valid_rows
        zero = jnp.zeros_like(r)
        r, v, a_neg = (jnp.where(keep, t, zero) for t in (r, v, a_neg))
        k_f, k_b, b_f, b_b = (jnp.where(keep, t, zero) for t in (k_f, k_b, b_f, b_b))
        lw_f, lw_b = (jnp.where(keep, t, zero) for t in (lw_f, lw_b))

    v_ref[0] = v.astype(BF16)
    g_ref[0] = g.astype(BF16)
    bonus_ref[0] = bonus.astype(BF16)

    ti = lax.broadcasted_iota(jnp.int32, (CHUNK, CHUNK), 0)
    si = lax.broadcasted_iota(jnp.int32, (CHUNK, CHUNK), 1)
    tri_f = jnp.where(si <= ti, 1.0, 0.0).astype(BF16)
    tri_b = jnp.where(si >= ti, 1.0, 0.0).astype(BF16)

    for c in range(tm // CHUNK):
        rows = slice(c * CHUNK, (c + 1) * CHUNK)
        r_c, v_c, a_c = r[rows], v[rows], a_neg[rows]
        for (lw_d, k_d, b_d, tri, fwd, ra_ref, kb_ref, bkt_ref, wc_ref) in (
                (lw_f, k_f, b_f, tri_f, True, raf_ref, kbf_ref, bktf_ref, wcf_ref),
                (lw_b, k_b, b_b, tri_b, False, rab_ref, kbb_ref, bktb_ref, wcb_ref)):
            lw_c = lw_d[rows]
            cum = _tri_dot(tri, lw_c)
            tot = cum[CHUNK - 1:CHUNK] if fwd else cum[0:1]
            e_excl = jnp.exp2(cum - lw_c)
            e_read = jnp.exp2(cum) if fwd else e_excl
            e_inv = jnp.exp2(-cum)
            e_end = jnp.exp2(tot - cum)
            ra_ref[0, 2 * c * CHUNK:(2 * c + 1) * CHUNK] = (r_c * e_read).astype(BF16)
            ra_ref[0, (2 * c + 1) * CHUNK:(2 * c + 2) * CHUNK] = (a_c * e_excl).astype(BF16)
            kb_ref[0, 2 * c * CHUNK:(2 * c + 1) * CHUNK] = (k_d[rows] * e_inv).astype(BF16)
            kb_ref[0, (2 * c + 1) * CHUNK:(2 * c + 2) * CHUNK] = (b_d[rows] * e_inv).astype(BF16)
            bk = jnp.concatenate([b_d[rows] * e_end, k_d[rows] * e_end], axis=0)
            bkt_ref[0, c * w:(c + 1) * w] = bk.T.astype(BF16)
            wc_ref[0, 8 * c:8 * (c + 1)] = jnp.broadcast_to(jnp.exp2(tot), (8, w))


def _prep(p, prev8, consts, tm, valid_rows):
    bsz, length, _ = p.shape
    nt = length // tm
    nc = length // CHUNK
    cpt = tm // CHUNK
    hb = tm // 8
    n8 = length // 8
    mup, mun, w0, w2, a0, a2, g2, k_k, k_a, r_k, gsum = consts
    tok = lambda b, i: (b, i, 0)
    dir_specs = [
        pl.BlockSpec((1, 2 * tm, RWKV_WIDTH), tok),
        pl.BlockSpec((1, 2 * tm, RWKV_WIDTH), tok),
        pl.BlockSpec((1, cpt * RWKV_WIDTH, PAIR), tok),
        pl.BlockSpec((1, cpt * 8, RWKV_WIDTH), tok),
    ]
    dir_shapes = [
        jax.ShapeDtypeStruct((bsz, 2 * length, RWKV_WIDTH), BF16),
        jax.ShapeDtypeStruct((bsz, 2 * length, RWKV_WIDTH), BF16),
        jax.ShapeDtypeStruct((bsz, nc * RWKV_WIDTH, PAIR), BF16),
        jax.ShapeDtypeStruct((bsz, nc * 8, RWKV_WIDTH), F32),
    ]
    tok_spec = pl.BlockSpec((1, tm, RWKV_WIDTH), tok)
    tok_shape = jax.ShapeDtypeStruct((bsz, length, RWKV_WIDTH), BF16)
    return pl.pallas_call(
        functools.partial(_prep_kernel, valid_rows=valid_rows),
        grid=(bsz, nt),
        in_specs=[
            pl.BlockSpec((1, tm, SHIFT_COLS), tok),
            pl.BlockSpec((1, 8, SHIFT_COLS), lambda b, i: (b, jnp.maximum(i * hb - 1, 0), 0)),
            pl.BlockSpec((1, 8, SHIFT_COLS), lambda b, i: (b, jnp.minimum((i + 1) * hb, n8 - 1), 0)),
            _const_spec((8, SHIFT_COLS)),
            _const_spec((1, SHIFT_COLS)),
            _const_spec((1, SHIFT_COLS)),
            _const_spec((1, 2 * RWKV_WIDTH)),
            _const_spec((2, LORA_PAIR, 2 * RWKV_WIDTH)),
            _const_spec((1, 2 * RWKV_WIDTH)),
            _const_spec((2, LORA_PAIR, 2 * RWKV_WIDTH)),
            _const_spec((GATE_LORA, RWKV_WIDTH)),
            _const_spec((1, RWKV_WIDTH)),
            _const_spec((1, RWKV_WIDTH)),
            _const_spec((1, RWKV_WIDTH)),
            _const_spec((PAIR, PAIR)),
        ],
        out_specs=dir_specs + dir_specs + [tok_spec, tok_spec, tok_spec],
        out_shape=dir_shapes + dir_shapes + [tok_shape, tok_shape, tok_shape],
        compiler_params=_cparams(("parallel", "parallel")),
        name="rwkv_prep",
    )(p, p, p, prev8, mup, mun, w0, w2, a0, a2, g2, k_k, k_a, r_k, gsum)


def _row_blocks(x, lane_lo):
    zero = jnp.zeros_like(x)
    return jnp.concatenate([jnp.where(lane_lo, x, zero), jnp.where(lane_lo, zero, x)], axis=0)


def _scan_chains(dirs, cps):
    ti = lax.broadcasted_iota(jnp.int32, (CHUNK, PAIR), 0)
    li = lax.broadcasted_iota(jnp.int32, (CHUNK, PAIR), 1)
    si = li % CHUNK
    lane_lo = li < CHUNK
    lower, lower_eq, upper = ti > si, ti >= si, ti < si
    same_sub = (ti // SUB) == (si // SUB)
    eye_w = jnp.where(ti == si, 1.0, 0.0).astype(F32)
    rr = lax.broadcasted_iota(jnp.int32, (PAIR, PAIR), 0)
    cc = lax.broadcasted_iota(jnp.int32, (PAIR, PAIR), 1)
    same_head = (rr // RWKV_HEAD) == (cc // RWKV_HEAD)
    diag = rr == cc
    rb = lambda t: _row_blocks(t, lane_lo)
    each = lambda fn, *cols: [fn(*vals) for vals in zip(*cols)]

    chains = [(d, j, u) for u in range(cps) for d in dirs for j in range(N_PAIRS)]
    lanes = [slice(j * PAIR, (j + 1) * PAIR) for _, j, _ in chains]
    mask_a = [lower if d[7] else upper for d, _, _ in chains]
    mask_r = [lower_eq if d[7] else upper for d, _, _ in chains]
    ra = [d[0][0, 2 * u * CHUNK:2 * (u + 1) * CHUNK, ln] for (d, _, u), ln in zip(chains, lanes)]
    kb = [d[1][0, 2 * u * CHUNK:2 * (u + 1) * CHUNK, ln] for (d, _, u), ln in zip(chains, lanes)]
    v2 = [d[4][0, u * CHUNK:(u + 1) * CHUNK, ln] for (d, _, u), ln in zip(chains, lanes)]
    bkt = [d[2][0, u * RWKV_WIDTH + j * PAIR:u * RWKV_WIDTH + (j + 1) * PAIR, :] for d, j, u in chains]
    wc = [d[3][0, 8 * u:8 * u + 1, ln] for (d, _, u), ln in zip(chains, lanes)]
    rbv = each(rb, v2)

    rows2 = lambda a, b: jnp.concatenate([a, b], axis=0)
    cols2 = lambda a, b: jnp.concatenate([a, b], axis=1)
    top, bot = (lambda t: t[:CHUNK]), (lambda t: t[CHUNK:])
    left, right = (lambda t: t[:, :PAIR]), (lambda t: t[:, PAIR:])

    akb = each(lambda a, b: _mm_nt(a, rows2(rb(b[:CHUNK]), rb(b[CHUNK:]))), ra, kb)
    a_rk = each(lambda m, t: jnp.where(m, left(top(t)), 0.0), mask_r, akb)
    a_ak = each(lambda m, t: jnp.where(m, left(bot(t)), 0.0), mask_a, akb)
    a_rb = each(lambda m, t: jnp.where(m, right(top(t)), 0.0), mask_r, akb)
    n_all = each(lambda m, t: jnp.where(m, right(bot(t)), 0.0), mask_a, akb)

    d1 = each(lambda t: jnp.where(same_sub, t, 0.0), n_all)
    e1 = each(lambda t, d: t - d, n_all, d1)
    d2 = each(lambda d: _mm(d, rb(d)), d1)
    arav = each(lambda a, b, r: _mm(rows2(a, b), r), a_rk, a_ak, rbv)
    t16 = each(lambda d: eye_w + d, d1)
    x2 = each(lambda t, d: _mm(rows2(t, d), rb(d)), t16, d2)
    t16 = each(lambda t, x: t + top(x), t16, x2)
    x4 = each(lambda t, x: _mm(rows2(t, bot(x)), rb(bot(x))), t16, x2)
    t16 = each(lambda t, x: t + top(x), t16, x4)
    t16 = each(lambda t, x: t + _mm(t, rb(bot(x))), t16, x4)
    m1 = each(lambda t, e: _mm(t, rb(e)), t16, e1)
    x6 = each(lambda m, t: _mm(m, cols2(rb(m), rb(t))), m1, t16)
    yy = each(lambda t, x: t + right(x), t16, x6)
    tinv = each(lambda y, x: y + _mm(left(x), rb(y)), yy, x6)

    tu = each(lambda t, r, a: _mm(t, cols2(rb(bot(r)), rb(bot(a).astype(BF16)))), tinv, ra, arav)
    ar = each(lambda a, t: _mm(a, cols2(rb(left(t)), rb(right(t)))), a_rb, tu)
    rp = each(lambda r, a: (top(r).astype(F32) + left(a)).astype(BF16), ra, ar)
    y0 = each(lambda a, w: right(a) + top(w), ar, arav)
    stack = each(lambda t, v: jnp.concatenate(
        [t, jnp.concatenate([jnp.zeros((CHUNK, PAIR), F32), v.astype(F32)], axis=1)], axis=0), tu, v2)
    pd = each(_mm, bkt, stack)
    p2 = each(lambda t, w: _split2(jnp.where(same_head, t[:, :PAIR], 0.0) + jnp.where(diag, w, 0.0)), pd, wc)
    d02 = each(lambda t: jnp.where(same_head, t[:, PAIR:], 0.0), pd)

    state = {(id(d), j): d[6][j] for d in dirs for j in range(N_PAIRS)}
    for step in range(cps):
        for di, d in enumerate(dirs):
            u = step if d[7] else cps - 1 - step
            for j in range(N_PAIRS):
                c = (u * len(dirs) + di) * N_PAIRS + j
                s_hi, s_lo = _split2(state[(id(d), j)])
                p_hi, p_lo = p2[c]
                res = jnp.dot(jnp.concatenate([rp[c], p_hi, p_lo], axis=0), s_hi, preferred_element_type=F32)
                d[5][0, u * CHUNK:(u + 1) * CHUNK, lanes[c]] = res[:CHUNK] + y0[c]
                state[(id(d), j)] = (res[CHUNK:CHUNK + PAIR] + res[CHUNK + PAIR:]
                                     + jnp.dot(p_hi, s_lo, preferred_element_type=F32) + d02[c])
    for d in dirs:
        for j in range(N_PAIRS):
            d[6][j] = state[(id(d), j)]


def _scan_kernel(raf_ref, kbf_ref, bktf_ref, wcf_ref, vf_ref,
                 rab_ref, kbb_ref, bktb_ref, wcb_ref, vb_ref, s0_ref,
                 yf_ref, yb_ref, sout_ref, sf_ref, sb_ref, *, cps):
    i = pl.program_id(1)

    @pl.when(i == 0)
    def _():
        sf_ref[...] = s0_ref[0]
        sb_ref[...] = jnp.zeros_like(sb_ref)

    _scan_chains(((raf_ref, kbf_ref, bktf_ref, wcf_ref, vf_ref, yf_ref, sf_ref, True),
                  (rab_ref, kbb_ref, bktb_ref, wcb_ref, vb_ref, yb_ref, sb_ref, False)), cps)

    @pl.when(i == pl.num_programs(1) - 1)
    def _():
        sout_ref[0] = sf_ref[...]


def _scan(prep_out, s0, cps):
    raf, kbf, bktf, wcf, rab, kbb, bktb, wcb, v = prep_out[:9]
    bsz, length, _ = v.shape
    steps = length // (cps * CHUNK)
    fw = lambda b, i: (b, i, 0)
    bw = lambda b, i: (b, steps - 1 - i, 0)

    def dir_specs(im):
        return [
            pl.BlockSpec((1, 2 * cps * CHUNK, RWKV_WIDTH), im),
            pl.BlockSpec((1, 2 * cps * CHUNK, RWKV_WIDTH), im),
            pl.BlockSpec((1, cps * RWKV_WIDTH, PAIR), im),
            pl.BlockSpec((1, cps * 8, RWKV_WIDTH), im),
            pl.BlockSpec((1, cps * CHUNK, RWKV_WIDTH), im),
        ]

    state_spec = pl.BlockSpec((1, N_PAIRS, PAIR, PAIR), lambda b, i: (b, 0, 0, 0))
    return pl.pallas_call(
        functools.partial(_scan_kernel, cps=cps),
        grid=(bsz, steps),
        in_specs=dir_specs(fw) + dir_specs(bw) + [state_spec],
        out_specs=[
            pl.BlockSpec((1, cps * CHUNK, RWKV_WIDTH), fw),
            pl.BlockSpec((1, cps * CHUNK, RWKV_WIDTH), bw),
            state_spec,
        ],
        out_shape=[
            jax.ShapeDtypeStruct((bsz, length, RWKV_WIDTH), F32),
            jax.ShapeDtypeStruct((bsz, length, RWKV_WIDTH), F32),
            jax.ShapeDtypeStruct((bsz, N_PAIRS, PAIR, PAIR), F32),
        ],
        scratch_shapes=[pltpu.VMEM((N_PAIRS, PAIR, PAIR), F32), pltpu.VMEM((N_PAIRS, PAIR, PAIR), F32)],
        compiler_params=_cparams(("parallel", "arbitrary")),
        name="rwkv_scan",
    )(raf, kbf, bktf, wcf, v, rab, kbb, bktb, wcb, v, s0)


def _attn_kernel(q_ref, kt_ref, v_ref, ktm_ref, vm_ref, lq1_ref, lk1_ref, lq2_ref, lk2_ref, sg_ref,
                 o_ref, m_ref, l_ref, acc_ref, *, lam_init):
    kv = pl.program_id(2)
    tq = q_ref.shape[2]
    tk = kt_ref.shape[3]

    def lane_partial_sums(p):
        acc = p[:, :PAIR]
        for c in range(1, p.shape[1] // PAIR):
            acc = acc + p[:, c * PAIR:(c + 1) * PAIR]
        return acc

    @pl.when(kv == 0)
    def _():
        col = lax.broadcasted_iota(jnp.int32, (tq, META_PAD), 1)
        for hc in range(N_MAPS):
            s = jnp.dot(q_ref[0, hc], ktm_ref[0, hc], preferred_element_type=F32)
            s = jnp.where(col < N_META, s, NEG_BIG)
            m = jnp.max(s, axis=-1, keepdims=True)
            p = jnp.exp2(s - m)
            head = slice((hc // 2) * PAIR, (hc // 2 + 1) * PAIR)
            acc_ref[hc] = jnp.dot(p.astype(BF16), vm_ref[0, :, head], preferred_element_type=F32)
            l_ref[hc] = p
            m_ref[hc] = jnp.broadcast_to(m, (tq, PAIR))

    score = lambda hc: jnp.dot(q_ref[0, hc], kt_ref[0, hc], preferred_element_type=F32)
    s_next = score(0)
    for hc in range(N_MAPS):
        s = s_next
        if hc + 1 < N_MAPS:
            s_next = score(hc + 1)
        m_prev = m_ref[hc]
        m_next = jnp.maximum(m_prev, jnp.max(s, axis=-1, keepdims=True))
        alpha = jnp.exp2(m_prev - m_next)
        p = jnp.exp2(s - jnp.tile(m_next, (1, tk // PAIR)))
        head = slice((hc // 2) * PAIR, (hc // 2 + 1) * PAIR)
        l_ref[hc] = alpha * l_ref[hc] + lane_partial_sums(p)
        acc_ref[hc] = alpha * acc_ref[hc] + jnp.dot(p.astype(BF16), v_ref[0, :, head], preferred_element_type=F32)
        m_ref[hc] = m_next

    @pl.when(kv == pl.num_programs(2) - 1)
    def _():
        lam = (jnp.exp(jnp.sum(lq1_ref[...] * lk1_ref[...], axis=-1, keepdims=True))
               - jnp.exp(jnp.sum(lq2_ref[...] * lk2_ref[...], axis=-1, keepdims=True)) + lam_init)
        for hd in range(DIFF_HEADS):
            l1 = jnp.sum(l_ref[2 * hd], axis=-1, keepdims=True)
            l2 = jnp.sum(l_ref[2 * hd + 1], axis=-1, keepdims=True)
            o = acc_ref[2 * hd] / l1 - lam * (acc_ref[2 * hd + 1] / l2)
            ms = jnp.mean(o * o, axis=-1, keepdims=True)
            o = o * lax.rsqrt(ms + RMS_EPS) * sg_ref[...] * (1.0 - lam_init)
            o_ref[0, :, hd * PAIR:(hd + 1) * PAIR] = o.astype(BF16)


def _attention(q, kt, v, ktm, vm, lam_params, subln_g, lam_init, tq, tk):
    bsz, _, length, _ = q.shape
    lq1, lk1, lq2, lk2 = lam_params
    return pl.pallas_call(
        functools.partial(_attn_kernel, lam_init=lam_init),
        grid=(bsz, length // tq, length // tk),
        in_specs=[
            pl.BlockSpec((1, N_MAPS, tq, DIFF_HEAD), lambda b, i, j: (b, 0, i, 0)),
            pl.BlockSpec((1, N_MAPS, DIFF_HEAD, tk), lambda b, i, j: (b, 0, 0, j)),
            pl.BlockSpec((1, tk, DIFF_WIDTH), lambda b, i, j: (b, j, 0)),
            _const_spec((1, N_MAPS, DIFF_HEAD, META_PAD)),
            _const_spec((1, META_PAD, DIFF_WIDTH)),
            _const_spec((1, DIFF_HEAD)),
            _const_spec((1, DIFF_HEAD)),
            _const_spec((1, DIFF_HEAD)),
            _const_spec((1, DIFF_HEAD)),
            _const_spec((1, PAIR)),
        ],
        out_specs=pl.BlockSpec((1, tq, DIFF_WIDTH), lambda b, i, j: (b, i, 0)),
        out_shape=jax.ShapeDtypeStruct((bsz, length, DIFF_WIDTH), BF16),
        scratch_shapes=[pltpu.VMEM((N_MAPS, tq, PAIR), F32)] * 3,
        compiler_params=_cparams(("parallel", "parallel", "arbitrary")),
        name="diff_attn",
    )(q, kt, v, ktm, vm, lq1, lk1, lq2, lk2, subln_g)


def _out_kernel(x_ref, yf_ref, yb_ref, bonus_ref, g_ref, od_ref, lnw_ref, lnb_ref, gs_ref,
                wo_ref, g2n_ref, wg_ref, wu_ref, wd_ref, o_ref):
    gs = gs_ref[...]
    y = yf_ref[0] + yb_ref[0]
    mu = _split_dot(y, gs, 2) * (1.0 / RWKV_HEAD)
    yc = y - mu
    var = _split_dot(yc * yc, gs, 2) * (1.0 / RWKV_HEAD)
    yn = yc * lax.rsqrt(var + GN_EPS) * lnw_ref[...] + lnb_ref[...]
    o_rwkv = (yn + bonus_ref[0].astype(F32)) * g_ref[0].astype(F32)
    x1 = (x_ref[0]
          + jnp.dot(o_rwkv.astype(BF16), wo_ref[:RWKV_WIDTH, :], preferred_element_type=F32)
          + jnp.dot(od_ref[0], wo_ref[RWKV_WIDTH:, :], preferred_element_type=F32))
    ms = jnp.mean(x1 * x1, axis=-1, keepdims=True)
    h2 = (x1 * lax.rsqrt(ms + RMS_EPS) * g2n_ref[...]).astype(BF16)
    gate = jnp.dot(h2, wg_ref[...], preferred_element_type=F32)
    up = jnp.dot(h2, wu_ref[...], preferred_element_type=F32)
    act = (gate * jax.nn.sigmoid(gate) * up).astype(BF16)
    o_ref[0] = x1 + jnp.dot(act, wd_ref[...], preferred_element_type=F32)


def _out(x, yf, yb, bonus, g, od, consts, tm):
    bsz, length, _ = x.shape
    lnw, lnb, gsum, wo, g2n, wg, wu, wd = consts
    tok = lambda b, i: (b, i, 0)
    half = pl.BlockSpec((1, tm, RWKV_WIDTH), tok)
    return pl.pallas_call(
        _out_kernel,
        grid=(bsz, length // tm),
        in_specs=[
            pl.BlockSpec((1, tm, D_MODEL), tok), half, half, half, half, half,
            _const_spec((1, RWKV_WIDTH)),
            _const_spec((1, RWKV_WIDTH)),
            _const_spec((PAIR, PAIR)),
            _const_spec((D_MODEL, D_MODEL)),
            _const_spec((1, D_MODEL)),
            _const_spec((D_MODEL, D_FF)),
            _const_spec((D_MODEL, D_FF)),
            _const_spec((D_FF, D_MODEL)),
        ],
        out_specs=pl.BlockSpec((1, tm, D_MODEL), tok),
        out_shape=jax.ShapeDtypeStruct((bsz, length, D_MODEL), F32),
        compiler_params=_cparams(("parallel", "parallel")),
        name="out_ffn",
    )(x, yf, yb, bonus, g, od, lnw, lnb, gsum, wo, g2n, wg, wu, wd)


def _rope_tables(length, offset):
    pos = jnp.arange(length, dtype=F32) + offset
    inv = ROPE_THETA ** (-jnp.arange(0, DIFF_HEAD, 2, dtype=F32) / DIFF_HEAD)
    ang = pos[:, None] * inv[None, :]
    cos = jnp.concatenate([jnp.cos(ang)] * 4, axis=-1)
    sin = jnp.sin(ang)
    sin = jnp.concatenate([-sin, sin, -sin, sin], axis=-1)
    return cos, sin


def _block_diag2(a, b):
    za = jnp.zeros_like(a)
    return jnp.concatenate([jnp.concatenate([a, za], axis=1), jnp.concatenate([za, b], axis=1)], axis=0)


def _tile_for(length, pref):
    t = pref
    while length % t:
        t //= 2
    return t


def kernel(x_prompt, x_sample, meta_tokens, norm1_g, w_in, shift_mu_prev, shift_mu_next, w0_f, w2_f, w0_b, w2_b, a0_f, a2_f, a0_b, a2_b, g2, k_k, k_a, r_k, ln_x_w, ln_x_b, q_norm_g, k_norm_g, lam_q1, lam_k1, lam_q2, lam_k2, subln_g, w_out, norm2_g, w_gate, w_up, w_down):
    lam_init = 0.8 - 0.6 * math.exp(-0.3 * 0)
    row = lambda t: t.reshape(1, -1).astype(F32)
    gi = jnp.arange(PAIR) // RWKV_HEAD
    gsum = (gi[:, None] == gi[None, :]).astype(BF16)

    g1 = row(norm1_g[0])
    w_in_b = w_in[0].astype(BF16)
    qg = row(jnp.tile(q_norm_g[0], N_MAPS))
    kg = row(jnp.tile(k_norm_g[0], N_MAPS))
    prep_consts = (
        row(shift_mu_prev[0]), row(shift_mu_next[0]),
        row(jnp.concatenate([w0_f[0], w0_b[0]])), jnp.stack(_split2(_block_diag2(w2_f[0], w2_b[0]).astype(F32))),
        row(jnp.concatenate([a0_f[0], a0_b[0]])), jnp.stack(_split2(_block_diag2(a2_f[0], a2_b[0]).astype(F32))),
        g2[0].astype(BF16), row(k_k[0]), row(k_a[0]), row(r_k[0]), gsum,
    )
    lam_params = (row(lam_q1[0]), row(lam_k1[0]), row(lam_q2[0]), row(lam_k2[0]))
    sg = row(subln_g[0])
    out_consts = (row(ln_x_w[0]), row(ln_x_b[0]), gsum, w_out[0].astype(BF16), row(norm2_g[0]),
                  w_gate[0].astype(BF16), w_up[0].astype(BF16), w_down[0].astype(BF16))

    meta_x = jnp.zeros((1, META_PAD, D_MODEL), F32).at[0, :N_META].set(meta_tokens.astype(F32))
    cos_m, sin_m = _rope_tables(META_PAD, 0)
    p_meta, _, kt_meta, v_meta = _inproj(meta_x, cos_m, sin_m, g1, w_in_b, qg, kg, gsum, META_PAD)
    prev8 = p_meta[0, N_META - 8:N_META]
    zero8 = jnp.zeros_like(prev8)

    def group(x):
        bsz, length, _ = x.shape
        cos, sin = _rope_tables(length, N_META)
        p, q, kt, v = _inproj(x, cos, sin, g1, w_in_b, qg, kg, gsum, _tile_for(length, 512))
        pm = jnp.concatenate(
            [jnp.broadcast_to(p_meta[:, :N_META], (bsz, N_META, SHIFT_COLS)),
             p[:, :CHUNK - N_META]], axis=1)
        meta_prep = _prep(pm, zero8, prep_consts, CHUNK, N_META)
        zero_state = jnp.zeros((bsz, N_PAIRS, PAIR, PAIR), F32)
        s_meta = _scan(meta_prep, zero_state, 1)[2]
        prep_out = _prep(p, prev8, prep_consts, _tile_for(length, 256), None)
        yf, yb, _ = _scan(prep_out, s_meta, 4)
        od = _attention(q, kt, v, kt_meta, v_meta, lam_params, sg, lam_init,
                        _tile_for(length, 512), _tile_for(length, 1024))
        return _out(x, yf, yb, prep_out[10], prep_out[9], od, out_consts, _tile_for(length, 512))

    return (group(x_prompt), group(x_sample))
```

```python
import functools
import math

import jax
import jax.numpy as jnp
from jax import lax
from jax.experimental import pallas as pl
from jax.experimental.pallas import tpu as pltpu

F32 = jnp.float32
BF16 = jnp.bfloat16

D_MODEL = 1024
N_META = 16
RWKV_WIDTH = 512
RWKV_HEAD = 64
DIFF_WIDTH = 512
DIFF_HEAD = 64
DIFF_HEADS = 4
N_MAPS = 2 * DIFF_HEADS
GATE_LORA = 128
LORA_PAIR = 128
SHIFT_COLS = 3 * RWKV_WIDTH + GATE_LORA + 2 * LORA_PAIR
IN_COLS = SHIFT_COLS + 3 * DIFF_WIDTH
D_FF = 2816
ROPE_THETA = 10000.0
RMS_EPS = 1e-6
GN_EPS = 64e-5
CHUNK = 64
SUB = 16
PAIR = 128
N_PAIRS = RWKV_WIDTH // PAIR
META_PAD = 128
SCORES_AHEAD = 2
ONES_ROWS = 16
MAX_ROWS = 64
VMEM_LIMIT_BYTES = 56 * 1024 * 1024


def _cparams(sem):
    return pltpu.CompilerParams(dimension_semantics=sem, vmem_limit_bytes=VMEM_LIMIT_BYTES)


def _const_spec(shape):
    nd = len(shape)
    return pl.BlockSpec(shape, lambda *_: (0,) * nd, pipeline_mode=pl.Buffered(1))


def _mm(a, b):
    return jnp.dot(a.astype(BF16), b.astype(BF16), preferred_element_type=F32)


def _mm_nt(a, b):
    return lax.dot_general(a.astype(BF16), b.astype(BF16), (((1,), (1,)), ((), ())),
                           preferred_element_type=F32)


def _split_dot(x, m, terms):
    outs = []
    for c in range(x.shape[1] // PAIR):
        acc = None
        rem = x[:, c * PAIR:(c + 1) * PAIR]
        for t in range(terms):
            piece = rem.astype(BF16)
            part = jnp.dot(piece, m, preferred_element_type=F32)
            acc = part if acc is None else acc + part
            if t + 1 < terms:
                rem = rem - piece.astype(F32)
        outs.append(acc)
    return jnp.concatenate(outs, axis=1)


def _split2(x):
    hi = x.astype(BF16)
    return hi, (x - hi.astype(F32)).astype(BF16)


def _mm3(a_hi, a_lo, b_hi, b_lo):
    dot = functools.partial(jnp.dot, preferred_element_type=F32)
    return dot(a_hi, b_hi) + (dot(a_lo, b_hi) + dot(a_hi, b_lo))


def _tri_dot(tri, x):
    acc = None
    rem = x
    for _ in range(3):
        piece = rem.astype(BF16)
        part = jnp.dot(tri, piece, preferred_element_type=F32)
        acc = part if acc is None else acc + part
        rem = rem - piece.astype(F32)
    return acc


def _inproj_kernel(x_ref, g1_ref, w_ref, cos_ref, sin_ref, qg_ref, kg_ref, gs_ref,
                   p_ref, qt_ref, k_ref, vt_ref):
    x = x_ref[0]
    ms = jnp.mean(x * x, axis=-1, keepdims=True)
    h = (x * lax.rsqrt(ms + RMS_EPS) * g1_ref[...]).astype(BF16)
    p_ref[0] = jnp.dot(h, w_ref[:, :SHIFT_COLS], preferred_element_type=F32)
    qkv = jnp.dot(h, w_ref[:, SHIFT_COLS:], preferred_element_type=F32)
    q = qkv[:, :DIFF_WIDTH]
    k = qkv[:, DIFF_WIDTH:2 * DIFF_WIDTH]
    v = qkv[:, 2 * DIFF_WIDTH:]

    reps = DIFF_WIDTH // cos_ref.shape[1]
    cos = jnp.tile(cos_ref[...], (1, reps))
    sin = jnp.tile(sin_ref[...], (1, reps))
    lane = lax.broadcasted_iota(jnp.int32, q.shape, 1)
    first_half = (lane % DIFF_HEAD) < (DIFF_HEAD // 2)
    gs = gs_ref[...]

    def norm_rope(t, g):
        ss = _split_dot(t * t, gs, 1)
        t = t * lax.rsqrt(ss * (1.0 / DIFF_HEAD) + RMS_EPS) * g
        rot = jnp.where(first_half,
                        pltpu.roll(t, DIFF_WIDTH - DIFF_HEAD // 2, 1),
                        pltpu.roll(t, DIFF_HEAD // 2, 1))
        return t * cos + rot * sin

    qr = norm_rope(q, qg_ref[...]) * (math.log2(math.e) / math.sqrt(DIFF_HEAD))
    kr = norm_rope(k, kg_ref[...])
    qt_ref[0] = qr.T.astype(BF16)
    k_ref[0] = kr.astype(BF16)
    vt_ref[0] = v.T.astype(BF16)


def _inproj(x, cos, sin, g1, w_in, qg, kg, gsum, tm):
    bsz, length, _ = x.shape
    grid = (bsz, length // tm)
    return pl.pallas_call(
        _inproj_kernel,
        grid=grid,
        in_specs=[
            pl.BlockSpec((1, tm, D_MODEL), lambda b, i: (b, i, 0)),
            _const_spec((1, D_MODEL)),
            _const_spec((D_MODEL, IN_COLS)),
            pl.BlockSpec((tm, PAIR), lambda b, i: (i, 0)),
            pl.BlockSpec((tm, PAIR), lambda b, i: (i, 0)),
            _const_spec((1, DIFF_WIDTH)),
            _const_spec((1, DIFF_WIDTH)),
            _const_spec((PAIR, PAIR)),
        ],
        out_specs=[
            pl.BlockSpec((1, tm, SHIFT_COLS), lambda b, i: (b, i, 0)),
            pl.BlockSpec((1, DIFF_WIDTH, tm), lambda b, i: (b, 0, i)),
            pl.BlockSpec((1, tm, DIFF_WIDTH), lambda b, i: (b, i, 0)),
            pl.BlockSpec((1, DIFF_WIDTH, tm), lambda b, i: (b, 0, i)),
        ],
        out_shape=[
            jax.ShapeDtypeStruct((bsz, length, SHIFT_COLS), F32),
            jax.ShapeDtypeStruct((bsz, DIFF_WIDTH, length), BF16),
            jax.ShapeDtypeStruct((bsz, length, DIFF_WIDTH), BF16),
            jax.ShapeDtypeStruct((bsz, DIFF_WIDTH, length), BF16),
        ],
        compiler_params=_cparams(("parallel", "parallel")),
        name="inproj",
    )(x, g1, w_in, cos, sin, qg, kg, gsum)


def _prep_kernel(p_ref, ph_ref, nh_ref, p0_ref, mup_ref, mun_ref, w0_ref, w2_ref, a0_ref, a2_ref,
                 g2_ref, kk_ref, ka_ref, rk_ref, gs_ref,
                 raf_ref, kbf_ref, bktf_ref, wcf_ref, rab_ref, kbb_ref, bktb_ref, wcb_ref,
                 v_ref, g_ref, bonus_ref, *, valid_rows):
    i = pl.program_id(1)
    last = pl.num_programs(1) - 1
    p = p_ref[0]
    tm = p.shape[0]
    row = lax.broadcasted_iota(jnp.int32, p.shape, 0)
    prev_row = jnp.where(i == 0, p0_ref[7:8, :], ph_ref[0, 7:8, :])
    next_row = jnp.where(i == last, jnp.zeros_like(prev_row), nh_ref[0, 0:1, :])
    p_prev = jnp.where(row == 0, prev_row, pltpu.roll(p, 1, 0))
    p_next = jnp.where(row == tm - 1, next_row, pltpu.roll(p, tm - 1, 0))
    z = p + mup_ref[...] * (p_prev - p) + mun_ref[...] * (p_next - p)

    w = RWKV_WIDTH
    r = z[:, :w]
    k = z[:, w:2 * w]
    v = z[:, 2 * w:3 * w]
    gd = z[:, 3 * w:3 * w + GATE_LORA]
    wd = z[:, 3 * w + GATE_LORA:3 * w + GATE_LORA + LORA_PAIR]
    ad = z[:, 3 * w + GATE_LORA + LORA_PAIR:]

    wl = w0_ref[...] + _mm3(*_split2(jnp.tanh(wd)), w2_ref[0], w2_ref[1])
    lw = (-math.exp(-0.5) * math.log2(math.e)) * jax.nn.sigmoid(wl)
    al = a0_ref[...] + _mm3(*_split2(ad), a2_ref[0], a2_ref[1])
    iclr = jax.nn.sigmoid(al)
    g = _mm(jax.nn.sigmoid(gd), g2_ref[...])

    gs = gs_ref[...]
    kk = k * kk_ref[...]
    kk = kk * lax.rsqrt(jnp.maximum(_split_dot(kk * kk, gs, 2), 1e-24))
    k_a = ka_ref[...]
    a_f = iclr[:, :w]
    a_b = iclr[:, w:]
    k_f = k * (1.0 + (a_f - 1.0) * k_a)
    k_b = k * (1.0 + (a_b - 1.0) * k_a)
    bonus = _split_dot(r * k_f * rk_ref[...], gs, 2) * v
    a_neg = -kk
    b_f = kk * a_f
    b_b = kk * a_b
    lw_f = lw[:, :w]
    lw_b = lw[:, w:]

    if valid_rows is not None:
        keep = lax.broadcasted_iota(jnp.int32, r.shape, 0) < valid_rows
        zero = jnp.zeros_like(r)
        r, v, a_neg = (jnp.where(keep, t, zero) for t in (r, v, a_neg))
        k_f, k_b, b_f, b_b = (jnp.where(keep, t, zero) for t in (k_f, k_b, b_f, b_b))
        lw_f, lw_b = (jnp.where(keep, t, zero) for t in (lw_f, lw_b))

    v_ref[0] = v.astype(BF16)
    g_ref[0] = g.astype(BF16)
    bonus_ref[0] = bonus.astype(BF16)

    ti = lax.broadcasted_iota(jnp.int32, (CHUNK, CHUNK), 0)
    si = lax.broadcasted_iota(jnp.int32, (CHUNK, CHUNK), 1)
    tri_f = jnp.where(si <= ti, 1.0, 0.0).astype(BF16)
    tri_b = jnp.where(si >= ti, 1.0, 0.0).astype(BF16)

    for c in range(tm // CHUNK):
        rows = slice(c * CHUNK, (c + 1) * CHUNK)
        r_c, v_c, a_c = r[rows], v[rows], a_neg[rows]
        for (lw_d, k_d, b_d, tri, fwd, ra_ref, kb_ref, bkt_ref, wc_ref) in (
                (lw_f, k_f, b_f, tri_f, True, raf_ref, kbf_ref, bktf_ref, wcf_ref),
                (lw_b, k_b, b_b, tri_b, False, rab_ref, kbb_ref, bktb_ref, wcb_ref)):
            lw_c = lw_d[rows]
            cum = _tri_dot(tri, lw_c)
            tot = cum[CHUNK - 1:CHUNK] if fwd else cum[0:1]
            e_excl = jnp.exp2(cum - lw_c)
            e_read = jnp.exp2(cum) if fwd else e_excl
            e_inv = jnp.exp2(-cum)
            e_end = jnp.exp2(tot - cum)
            ra_ref[0, 2 * c * CHUNK:(2 * c + 1) * CHUNK] = (r_c * e_read).astype(BF16)
            ra_ref[0, (2 * c + 1) * CHUNK:(2 * c + 2) * CHUNK] = (a_c * e_excl).astype(BF16)
            kb_ref[0, 2 * c * CHUNK:(2 * c + 1) * CHUNK] = (k_d[rows] * e_inv).astype(BF16)
            kb_ref[0, (2 * c + 1) * CHUNK:(2 * c + 2) * CHUNK] = (b_d[rows] * e_inv).astype(BF16)
            bk = jnp.concatenate([b_d[rows] * e_end, k_d[rows] * e_end], axis=0)
            bkt_ref[0, c * w:(c + 1) * w] = bk.T.astype(BF16)
            wc_ref[0, 8 * c:8 * (c + 1)] = jnp.broadcast_to(jnp.exp2(tot), (8, w))


def _prep(p, prev8, consts, tm, valid_rows):
    bsz, length, _ = p.shape
    nt = length // tm
    nc = length // CHUNK
    cpt = tm // CHUNK
    hb = tm // 8
    n8 = length // 8
    mup, mun, w0, w2, a0, a2, g2, k_k, k_a, r_k, gsum = consts
    tok = lambda b, i: (b, i, 0)
    dir_specs = [
        pl.BlockSpec((1, 2 * tm, RWKV_WIDTH), tok),
        pl.BlockSpec((1, 2 * tm, RWKV_WIDTH), tok),
        pl.BlockSpec((1, cpt * RWKV_WIDTH, PAIR), tok),
        pl.BlockSpec((1, cpt * 8, RWKV_WIDTH), tok),
    ]
    dir_shapes = [
        jax.ShapeDtypeStruct((bsz, 2 * length, RWKV_WIDTH), BF16),
        jax.ShapeDtypeStruct((bsz, 2 * length, RWKV_WIDTH), BF16),
        jax.ShapeDtypeStruct((bsz, nc * RWKV_WIDTH, PAIR), BF16),
        jax.ShapeDtypeStruct((bsz, nc * 8, RWKV_WIDTH), F32),
    ]
    tok_spec = pl.BlockSpec((1, tm, RWKV_WIDTH), tok)
    tok_shape = jax.ShapeDtypeStruct((bsz, length, RWKV_WIDTH), BF16)
    return pl.pallas_call(
        functools.partial(_prep_kernel, valid_rows=valid_rows),
        grid=(bsz, nt),
        in_specs=[
            pl.BlockSpec((1, tm, SHIFT_COLS), tok),
            pl.BlockSpec((1, 8, SHIFT_COLS), lambda b, i: (b, jnp.maximum(i * hb - 1, 0), 0)),
            pl.BlockSpec((1, 8, SHIFT_COLS), lambda b, i: (b, jnp.minimum((i + 1) * hb, n8 - 1), 0)),
            _const_spec((8, SHIFT_COLS)),
            _const_spec((1, SHIFT_COLS)),
            _const_spec((1, SHIFT_COLS)),
            _const_spec((1, 2 * RWKV_WIDTH)),
            _const_spec((2, LORA_PAIR, 2 * RWKV_WIDTH)),
            _const_spec((1, 2 * RWKV_WIDTH)),
            _const_spec((2, LORA_PAIR, 2 * RWKV_WIDTH)),
            _const_spec((GATE_LORA, RWKV_WIDTH)),
            _const_spec((1, RWKV_WIDTH)),
            _const_spec((1, RWKV_WIDTH)),
            _const_spec((1, RWKV_WIDTH)),
            _const_spec((PAIR, PAIR)),
        ],
        out_specs=dir_specs + dir_specs + [tok_spec, tok_spec, tok_spec],
        out_shape=dir_shapes + dir_shapes + [tok_shape, tok_shape, tok_shape],
        compiler_params=_cparams(("parallel", "parallel")),
        name="rwkv_prep",
    )(p, p, p, prev8, mup, mun, w0, w2, a0, a2, g2, k_k, k_a, r_k, gsum)


def _row_blocks(x, lane_lo):
    zero = jnp.zeros_like(x)
    return jnp.concatenate([jnp.where(lane_lo, x, zero), jnp.where(lane_lo, zero, x)], axis=0)


def _scan_chains(dirs, cps):
    ti = lax.broadcasted_iota(jnp.int32, (CHUNK, PAIR), 0)
    li = lax.broadcasted_iota(jnp.int32, (CHUNK, PAIR), 1)
    si = li % CHUNK
    lane_lo = li < CHUNK
    lower, lower_eq, upper = ti > si, ti >= si, ti < si
    same_sub = (ti // SUB) == (si // SUB)
    eye_w = jnp.where(ti == si, 1.0, 0.0).astype(F32)
    rr = lax.broadcasted_iota(jnp.int32, (PAIR, PAIR), 0)
    cc = lax.broadcasted_iota(jnp.int32, (PAIR, PAIR), 1)
    same_head = (rr // RWKV_HEAD) == (cc // RWKV_HEAD)
    diag = rr == cc
    rb = lambda t: _row_blocks(t, lane_lo)
    each = lambda fn, *cols: [fn(*vals) for vals in zip(*cols)]

    chains = [(d, j, u) for u in range(cps) for d in dirs for j in range(N_PAIRS)]
    lanes = [slice(j * PAIR, (j + 1) * PAIR) for _, j, _ in chains]
    mask_a = [lower if d[7] else upper for d, _, _ in chains]
    mask_r = [lower_eq if d[7] else upper for d, _, _ in chains]
    ra = [d[0][0, 2 * u * CHUNK:2 * (u + 1) * CHUNK, ln] for (d, _, u), ln in zip(chains, lanes)]
    kb = [d[1][0, 2 * u * CHUNK:2 * (u + 1) * CHUNK, ln] for (d, _, u), ln in zip(chains, lanes)]
    v2 = [d[4][0, u * CHUNK:(u + 1) * CHUNK, ln] for (d, _, u), ln in zip(chains, lanes)]
    bkt = [d[2][0, u * RWKV_WIDTH + j * PAIR:u * RWKV_WIDTH + (j + 1) * PAIR, :] for d, j, u in chains]
    wc = [d[3][0, 8 * u:8 * u + 1, ln] for (d, _, u), ln in zip(chains, lanes)]
    rbv = each(rb, v2)

    rows2 = lambda a, b: jnp.concatenate([a, b], axis=0)
    cols2 = lambda a, b: jnp.concatenate([a, b], axis=1)
    top, bot = (lambda t: t[:CHUNK]), (lambda t: t[CHUNK:])
    left, right = (lambda t: t[:, :PAIR]), (lambda t: t[:, PAIR:])

    akb = each(lambda a, b: _mm_nt(a, rows2(rb(b[:CHUNK]), rb(b[CHUNK:]))), ra, kb)
    a_rk = each(lambda m, t: jnp.where(m, left(top(t)), 0.0), mask_r, akb)
    a_ak = each(lambda m, t: jnp.where(m, left(bot(t)), 0.0), mask_a, akb)
    a_rb = each(lambda m, t: jnp.where(m, right(top(t)), 0.0), mask_r, akb)
    n_all = each(lambda m, t: jnp.where(m, right(bot(t)), 0.0), mask_a, akb)

    d1 = each(lambda t: jnp.where(same_sub, t, 0.0), n_all)
    e1 = each(lambda t, d: t - d, n_all, d1)
    d2 = each(lambda d: _mm(d, rb(d)), d1)
    arav = each(lambda a, b, r: _mm(rows2(a, b), r), a_rk, a_ak, rbv)
    t16 = each(lambda d: eye_w + d, d1)
    x2 = each(lambda t, d: _mm(rows2(t, d), rb(d)), t16, d2)
    t16 = each(lambda t, x: t + top(x), t16, x2)
    x4 = each(lambda t, x: _mm(rows2(t, bot(x)), rb(bot(x))), t16, x2)
    t16 = each(lambda t, x: t + top(x), t16, x4)
    t16 = each(lambda t, x: t + _mm(t, rb(bot(x))), t16, x4)
    m1 = each(lambda t, e: _mm(t, rb(e)), t16, e1)
    x6 = each(lambda m, t: _mm(m, cols2(rb(m), rb(t))), m1, t16)
    yy = each(lambda t, x: t + right(x), t16, x6)
    tinv = each(lambda y, x: y + _mm(left(x), rb(y)), yy, x6)

    tu = each(lambda t, r, a: _mm(t, cols2(rb(bot(r)), rb(bot(a).astype(BF16)))), tinv, ra, arav)
    ar = each(lambda a, t: _mm(a, cols2(rb(left(t)), rb(right(t)))), a_rb, tu)
    rp = each(lambda r, a: (top(r).astype(F32) + left(a)).astype(BF16), ra, ar)
    y0 = each(lambda a, w: right(a) + top(w), ar, arav)
    stack = each(lambda t, v: jnp.concatenate(
        [t, jnp.concatenate([jnp.zeros((CHUNK, PAIR), F32), v.astype(F32)], axis=1)], axis=0), tu, v2)
    pd = each(_mm, bkt, stack)
    p2 = each(lambda t, w: _split2(jnp.where(same_head, t[:, :PAIR], 0.0) + jnp.where(diag, w, 0.0)), pd, wc)
    d02 = each(lambda t: jnp.where(same_head, t[:, PAIR:], 0.0), pd)

    state = {(id(d), j): d[6][j] for d in dirs for j in range(N_PAIRS)}
    for step in range(cps):
        for di, d in enumerate(dirs):
            u = step if d[7] else cps - 1 - step
            for j in range(N_PAIRS):
                c = (u * len(dirs) + di) * N_PAIRS + j
                s_hi, s_lo = _split2(state[(id(d), j)])
                p_hi, p_lo = p2[c]
                res = jnp.dot(jnp.concatenate([rp[c], p_hi, p_lo], axis=0), s_hi, preferred_element_type=F32)
                d[5][0, u * CHUNK:(u + 1) * CHUNK, lanes[c]] = res[:CHUNK] + y0[c]
                state[(id(d), j)] = (res[CHUNK:CHUNK + PAIR] + res[CHUNK + PAIR:]
                                     + jnp.dot(p_hi, s_lo, preferred_element_type=F32) + d02[c])
    for d in dirs:
        for j in range(N_PAIRS):
            d[6][j] = state[(id(d), j)]


def _scan_kernel(raf_ref, kbf_ref, bktf_ref, wcf_ref, vf_ref,
                 rab_ref, kbb_ref, bktb_ref, wcb_ref, vb_ref, s0_ref,
                 yf_ref, yb_ref, sout_ref, sf_ref, sb_ref, *, cps):
    i = pl.program_id(1)

    @pl.when(i == 0)
    def _():
        sf_ref[...] = s0_ref[0]
        sb_ref[...] = jnp.zeros_like(sb_ref)

    _scan_chains(((raf_ref, kbf_ref, bktf_ref, wcf_ref, vf_ref, yf_ref, sf_ref, True),
                  (rab_ref, kbb_ref, bktb_ref, wcb_ref, vb_ref, yb_ref, sb_ref, False)), cps)

    @pl.when(i == pl.num_programs(1) - 1)
    def _():
        sout_ref[0] = sf_ref[...]


def _scan(prep_out, s0, cps):
    raf, kbf, bktf, wcf, rab, kbb, bktb, wcb, v = prep_out[:9]
    bsz, length, _ = v.shape
    steps = length // (cps * CHUNK)
    fw = lambda b, i: (b, i, 0)
    bw = lambda b, i: (b, steps - 1 - i, 0)

    def dir_specs(im):
        return [
            pl.BlockSpec((1, 2 * cps * CHUNK, RWKV_WIDTH), im),
            pl.BlockSpec((1, 2 * cps * CHUNK, RWKV_WIDTH), im),
            pl.BlockSpec((1, cps * RWKV_WIDTH, PAIR), im),
            pl.BlockSpec((1, cps * 8, RWKV_WIDTH), im),
            pl.BlockSpec((1, cps * CHUNK, RWKV_WIDTH), im),
        ]

    state_spec = pl.BlockSpec((1, N_PAIRS, PAIR, PAIR), lambda b, i: (b, 0, 0, 0))
    return pl.pallas_call(
        functools.partial(_scan_kernel, cps=cps),
        grid=(bsz, steps),
        in_specs=dir_specs(fw) + dir_specs(bw) + [state_spec],
        out_specs=[
            pl.BlockSpec((1, cps * CHUNK, RWKV_WIDTH), fw),
            pl.BlockSpec((1, cps * CHUNK, RWKV_WIDTH), bw),
            state_spec,
        ],
        out_shape=[
            jax.ShapeDtypeStruct((bsz, length, RWKV_WIDTH), F32),
            jax.ShapeDtypeStruct((bsz, length, RWKV_WIDTH), F32),
            jax.ShapeDtypeStruct((bsz, N_PAIRS, PAIR, PAIR), F32),
        ],
        scratch_shapes=[pltpu.VMEM((N_PAIRS, PAIR, PAIR), F32), pltpu.VMEM((N_PAIRS, PAIR, PAIR), F32)],
        compiler_params=_cparams(("parallel", "arbitrary")),
        name="rwkv_scan",
    )(raf, kbf, bktf, wcf, v, rab, kbb, bktb, wcb, v, s0)


def _attn_kernel(k_ref, qt_ref, vt_ref, km_ref, vtm_ref, lq1_ref, lk1_ref, lq2_ref, lk2_ref, sg_ref,
                 o_ref, m_ref, acc_ref, *, lam_init):
    kv = pl.program_id(2)
    tk = k_ref.shape[1]
    tq = qt_ref.shape[2]
    map_row = lax.broadcasted_iota(jnp.int32, (PAIR, tq), 0) < DIFF_HEAD

    def q_weights(hc):
        qt2 = qt_ref[0, (hc // 2) * PAIR:(hc // 2 + 1) * PAIR, :]
        keep = map_row if hc % 2 == 0 else jnp.logical_not(map_row)
        return jnp.where(keep, qt2, jnp.zeros_like(qt2))

    def key_max(s):
        rows = min(MAX_ROWS, s.shape[0])
        part = jnp.max(s.reshape(s.shape[0] // rows, rows, s.shape[1]), axis=0)
        return jnp.max(part, axis=0, keepdims=True)

    def values_and_ones(vt_h):
        return jnp.concatenate([vt_h, jnp.ones((ONES_ROWS, vt_h.shape[1]), BF16)], axis=0)

    @pl.when(kv == 0)
    def _():
        heads = [slice((hc // 2) * PAIR, (hc // 2 + 1) * PAIR) for hc in range(N_MAPS)]
        s_meta = [jnp.dot(km_ref[0, :, heads[hc]], q_weights(hc), preferred_element_type=F32)
                  for hc in range(N_MAPS)]
        m_meta = [key_max(s) for s in s_meta]
        p_meta = [jnp.exp2(s - m).astype(BF16) for s, m in zip(s_meta, m_meta)]
        for hc in range(N_MAPS):
            acc_ref[hc] = jnp.dot(values_and_ones(vtm_ref[0, heads[hc], :]), p_meta[hc],
                                  preferred_element_type=F32)
            m_ref[hc] = jnp.broadcast_to(m_meta[hc], (8, tq))

    def score(hc):
        head = slice((hc // 2) * PAIR, (hc // 2 + 1) * PAIR)
        return jnp.dot(k_ref[0, :, head], q_weights(hc), preferred_element_type=F32)

    scores = {hc: score(hc) for hc in range(SCORES_AHEAD)}
    for hc in range(N_MAPS):
        if hc + SCORES_AHEAD < N_MAPS:
            scores[hc + SCORES_AHEAD] = score(hc + SCORES_AHEAD)
        s = scores.pop(hc)
        head = slice((hc // 2) * PAIR, (hc // 2 + 1) * PAIR)
        m_prev = m_ref[hc]
        m_next = jnp.maximum(m_prev, key_max(s))
        alpha = jnp.exp2(m_prev - m_next)
        p = jnp.exp2(s - m_next[0:1])
        acc_ref[hc] = alpha[0:1] * acc_ref[hc] + jnp.dot(values_and_ones(vt_ref[0, head, :]), p.astype(BF16),
                                                         preferred_element_type=F32)
        m_ref[hc] = m_next

    @pl.when(kv == pl.num_programs(2) - 1)
    def _():
        lam = (jnp.exp(jnp.sum(lq1_ref[...] * lk1_ref[...], axis=-1, keepdims=True))
               - jnp.exp(jnp.sum(lq2_ref[...] * lk2_ref[...], axis=-1, keepdims=True)) + lam_init)
        gain = jnp.tile(sg_ref[...], (1, tq // PAIR)) * (1.0 - lam_init)
        for hd in range(DIFF_HEADS):
            a1 = acc_ref[2 * hd]
            a2 = acc_ref[2 * hd + 1]
            o = a1[:PAIR] / a1[PAIR:PAIR + 1] - lam * (a2[:PAIR] / a2[PAIR:PAIR + 1])
            ms = jnp.mean(o * o, axis=0, keepdims=True)
            o = o * lax.rsqrt(ms + RMS_EPS) * gain
            o_ref[0, :, hd * PAIR:(hd + 1) * PAIR] = o.T.astype(BF16)


def _attention(k, qt, vt, km, vtm, lam_params, subln_g, lam_init, tq, tk):
    bsz, length, _ = k.shape
    lq1, lk1, lq2, lk2 = lam_params
    return pl.pallas_call(
        functools.partial(_attn_kernel, lam_init=lam_init),
        grid=(bsz, length // tq, length // tk),
        in_specs=[
            pl.BlockSpec((1, tk, DIFF_WIDTH), lambda b, i, j: (b, j, 0)),
            pl.BlockSpec((1, DIFF_WIDTH, tq), lambda b, i, j: (b, 0, i)),
            pl.BlockSpec((1, DIFF_WIDTH, tk), lambda b, i, j: (b, 0, j)),
            _const_spec((1, N_META, DIFF_WIDTH)),
            _const_spec((1, DIFF_WIDTH, N_META)),
            _const_spec((1, DIFF_HEAD)),
            _const_spec((1, DIFF_HEAD)),
            _const_spec((1, DIFF_HEAD)),
            _const_spec((1, DIFF_HEAD)),
            _const_spec((PAIR, PAIR)),
        ],
        out_specs=pl.BlockSpec((1, tq, DIFF_WIDTH), lambda b, i, j: (b, i, 0)),
        out_shape=jax.ShapeDtypeStruct((bsz, length, DIFF_WIDTH), BF16),
        scratch_shapes=[pltpu.VMEM((N_MAPS, 8, tq), F32), pltpu.VMEM((N_MAPS, PAIR + ONES_ROWS, tq), F32)],
        compiler_params=_cparams(("parallel", "parallel", "arbitrary")),
        name="diff_attn",
    )(k, qt, vt, km, vtm, lq1, lk1, lq2, lk2, subln_g)


def _out_kernel(x_ref, yf_ref, yb_ref, bonus_ref, g_ref, od_ref, lnw_ref, lnb_ref, gs_ref,
                wo_ref, g2n_ref, wg_ref, wu_ref, wd_ref, o_ref):
    gs = gs_ref[...]
    y = yf_ref[0] + yb_ref[0]
    mu = _split_dot(y, gs, 2) * (1.0 / RWKV_HEAD)
    yc = y - mu
    var = _split_dot(yc * yc, gs, 2) * (1.0 / RWKV_HEAD)
    yn = yc * lax.rsqrt(var + GN_EPS) * lnw_ref[...] + lnb_ref[...]
    o_rwkv = (yn + bonus_ref[0].astype(F32)) * g_ref[0].astype(F32)
    x1 = (x_ref[0]
          + jnp.dot(o_rwkv.astype(BF16), wo_ref[:RWKV_WIDTH, :], preferred_element_type=F32)
          + jnp.dot(od_ref[0], wo_ref[RWKV_WIDTH:, :], preferred_element_type=F32))
    ms = jnp.mean(x1 * x1, axis=-1, keepdims=True)
    h2 = (x1 * lax.rsqrt(ms + RMS_EPS) * g2n_ref[...]).astype(BF16)
    gate = jnp.dot(h2, wg_ref[...], preferred_element_type=F32)
    up = jnp.dot(h2, wu_ref[...], preferred_element_type=F32)
    act = (gate * jax.nn.sigmoid(gate) * up).astype(BF16)
    o_ref[0] = x1 + jnp.dot(act, wd_ref[...], preferred_element_type=F32)


def _out(x, yf, yb, bonus, g, od, consts, tm):
    bsz, length, _ = x.shape
    lnw, lnb, gsum, wo, g2n, wg, wu, wd = consts
    tok = lambda b, i: (b, i, 0)
    half = pl.BlockSpec((1, tm, RWKV_WIDTH), tok)
    return pl.pallas_call(
        _out_kernel,
        grid=(bsz, length // tm),
        in_specs=[
            pl.BlockSpec((1, tm, D_MODEL), tok), half, half, half, half, half,
            _const_spec((1, RWKV_WIDTH)),
            _const_spec((1, RWKV_WIDTH)),
            _const_spec((PAIR, PAIR)),
            _const_spec((D_MODEL, D_MODEL)),
            _const_spec((1, D_MODEL)),
            _const_spec((D_MODEL, D_FF)),
            _const_spec((D_MODEL, D_FF)),
            _const_spec((D_FF, D_MODEL)),
        ],
        out_specs=pl.BlockSpec((1, tm, D_MODEL), tok),
        out_shape=jax.ShapeDtypeStruct((bsz, length, D_MODEL), F32),
        compiler_params=_cparams(("parallel", "parallel")),
        name="out_ffn",
    )(x, yf, yb, bonus, g, od, lnw, lnb, gsum, wo, g2n, wg, wu, wd)


def _rope_tables(length, offset):
    pos = jnp.arange(length, dtype=F32) + offset
    inv = ROPE_THETA ** (-jnp.arange(0, DIFF_HEAD, 2, dtype=F32) / DIFF_HEAD)
    ang = pos[:, None] * inv[None, :]
    cos = jnp.concatenate([jnp.cos(ang)] * 4, axis=-1)
    sin = jnp.sin(ang)
    sin = jnp.concatenate([-sin, sin, -sin, sin], axis=-1)
    return cos, sin


def _block_diag2(a, b):
    za = jnp.zeros_like(a)
    return jnp.concatenate([jnp.concatenate([a, za], axis=1), jnp.concatenate([za, b], axis=1)], axis=0)


def _tile_for(length, pref):
    t = pref
    while length % t:
        t //= 2
    return t


def kernel(x_prompt, x_sample, meta_tokens, norm1_g, w_in, shift_mu_prev, shift_mu_next, w0_f, w2_f, w0_b, w2_b, a0_f, a2_f, a0_b, a2_b, g2, k_k, k_a, r_k, ln_x_w, ln_x_b, q_norm_g, k_norm_g, lam_q1, lam_k1, lam_q2, lam_k2, subln_g, w_out, norm2_g, w_gate, w_up, w_down):
    lam_init = 0.8 - 0.6 * math.exp(-0.3 * 0)
    row = lambda t: t.reshape(1, -1).astype(F32)
    gi = jnp.arange(PAIR) // RWKV_HEAD
    gsum = (gi[:, None] == gi[None, :]).astype(BF16)

    g1 = row(norm1_g[0])
    w_in_b = w_in[0].astype(BF16)
    qg = row(jnp.tile(q_norm_g[0], N_MAPS))
    kg = row(jnp.tile(k_norm_g[0], N_MAPS))
    prep_consts = (
        row(shift_mu_prev[0]), row(shift_mu_next[0]),
        row(jnp.concatenate([w0_f[0], w0_b[0]])), jnp.stack(_split2(_block_diag2(w2_f[0], w2_b[0]).astype(F32))),
        row(jnp.concatenate([a0_f[0], a0_b[0]])), jnp.stack(_split2(_block_diag2(a2_f[0], a2_b[0]).astype(F32))),
        g2[0].astype(BF16), row(k_k[0]), row(k_a[0]), row(r_k[0]), gsum,
    )
    lam_params = (row(lam_q1[0]), row(lam_k1[0]), row(lam_q2[0]), row(lam_k2[0]))
    sg = jnp.broadcast_to(subln_g[0].astype(F32)[:, None], (PAIR, PAIR))
    out_consts = (row(ln_x_w[0]), row(ln_x_b[0]), gsum, w_out[0].astype(BF16), row(norm2_g[0]),
                  w_gate[0].astype(BF16), w_up[0].astype(BF16), w_down[0].astype(BF16))

    meta_x = jnp.zeros((1, META_PAD, D_MODEL), F32).at[0, :N_META].set(meta_tokens.astype(F32))
    cos_m, sin_m = _rope_tables(META_PAD, 0)
    p_meta, _, k_meta, vt_meta = _inproj(meta_x, cos_m, sin_m, g1, w_in_b, qg, kg, gsum, META_PAD)
    prev8 = p_meta[0, N_META - 8:N_META]
    zero8 = jnp.zeros_like(prev8)

    def group(x):
        bsz, length, _ = x.shape
        cos, sin = _rope_tables(length, N_META)
        p, qt, k, vt = _inproj(x, cos, sin, g1, w_in_b, qg, kg, gsum, _tile_for(length, 512))
        pm = jnp.concatenate(
            [jnp.broadcast_to(p_meta[:, :N_META], (bsz, N_META, SHIFT_COLS)),
             p[:, :CHUNK - N_META]], axis=1)
        meta_prep = _prep(pm, zero8, prep_consts, CHUNK, N_META)
        zero_state = jnp.zeros((bsz, N_PAIRS, PAIR, PAIR), F32)
        s_meta = _scan(meta_prep, zero_state, 1)[2]
        prep_out = _prep(p, prev8, prep_consts, _tile_for(length, 256), None)
        yf, yb, _ = _scan(prep_out, s_meta, 4)
        od = _attention(k, qt, vt, k_meta[:, :N_META], vt_meta[:, :, :N_META], lam_params, sg, lam_init,
                        _tile_for(length, 512), _tile_for(length, 1024))
        return _out(x, yf, yb, prep_out[10], prep_out[9], od, out_consts, _tile_for(length, 512))

    return (group(x_prompt), group(x_sample))
```

```python
import functools
import math

import jax
import jax.numpy as jnp
from jax import lax
from jax.experimental import pallas as pl
from jax.experimental.pallas import tpu as pltpu

F32 = jnp.float32
BF16 = jnp.bfloat16

D_MODEL = 1024
N_META = 16
RWKV_WIDTH = 512
RWKV_HEAD = 64
DIFF_WIDTH = 512
DIFF_HEAD = 64
DIFF_HEADS = 4
N_MAPS = 2 * DIFF_HEADS
GATE_LORA = 128
LORA_PAIR = 128
SHIFT_COLS = 3 * RWKV_WIDTH + GATE_LORA + 2 * LORA_PAIR
IN_COLS = SHIFT_COLS + 3 * DIFF_WIDTH
D_FF = 2816
ROPE_THETA = 10000.0
RMS_EPS = 1e-6
GN_EPS = 64e-5
CHUNK = 64
SUB = 16
PAIR = 128
N_PAIRS = RWKV_WIDTH // PAIR
META_PAD = 128
SCORES_AHEAD = 2
ONES_ROWS = 16
MAX_ROWS = 64
VMEM_LIMIT_BYTES = 56 * 1024 * 1024


def _cparams(sem):
    return pltpu.CompilerParams(dimension_semantics=sem, vmem_limit_bytes=VMEM_LIMIT_BYTES)


def _const_spec(shape):
    nd = len(shape)
    return pl.BlockSpec(shape, lambda *_: (0,) * nd, pipeline_mode=pl.Buffered(1))


def _mm(a, b):
    return jnp.dot(a.astype(BF16), b.astype(BF16), preferred_element_type=F32)


def _mm_nt(a, b):
    return lax.dot_general(a.astype(BF16), b.astype(BF16), (((1,), (1,)), ((), ())),
                           preferred_element_type=F32)


def _split_dot(x, m, terms):
    outs = []
    for c in range(x.shape[1] // PAIR):
        acc = None
        rem = x[:, c * PAIR:(c + 1) * PAIR]
        for t in range(terms):
            piece = rem.astype(BF16)
            part = jnp.dot(piece, m, preferred_element_type=F32)
            acc = part if acc is None else acc + part
            if t + 1 < terms:
                rem = rem - piece.astype(F32)
        outs.append(acc)
    return jnp.concatenate(outs, axis=1)


def _split2(x):
    hi = x.astype(BF16)
    return hi, (x - hi.astype(F32)).astype(BF16)


def _mm3(a_hi, a_lo, b_hi, b_lo):
    dot = functools.partial(jnp.dot, preferred_element_type=F32)
    return dot(a_hi, b_hi) + (dot(a_lo, b_hi) + dot(a_hi, b_lo))


def _tri_dot(tri, x):
    hi, lo = _split2(x)
    return jnp.dot(tri, hi, preferred_element_type=F32) + jnp.dot(tri, lo, preferred_element_type=F32)


def _inproj_kernel(x_ref, g1_ref, w_ref, cos_a_ref, sin_a_ref, cos_b_ref, sin_b_ref, qg_ref, kg_ref, gs_ref,
                   p_ref, qt_ref, k_ref, vt_ref):
    x = x_ref[0]
    ms = jnp.mean(x * x, axis=-1, keepdims=True)
    h = (x * lax.rsqrt(ms + RMS_EPS) * g1_ref[...]).astype(BF16)
    p_ref[0] = jnp.dot(h, w_ref[:, :SHIFT_COLS], preferred_element_type=F32)
    qkv = jnp.dot(h, w_ref[:, SHIFT_COLS:], preferred_element_type=F32)
    q = qkv[:, :DIFF_WIDTH]
    k = qkv[:, DIFF_WIDTH:2 * DIFF_WIDTH]
    v = qkv[:, 2 * DIFF_WIDTH:]

    cos_a, sin_a, cos_b, sin_b = cos_a_ref[0], sin_a_ref[0], cos_b_ref[...], sin_b_ref[...]
    reps = DIFF_WIDTH // cos_b.shape[1]
    cos = jnp.tile(cos_a * cos_b - sin_a * sin_b, (1, reps))
    sin = jnp.tile(sin_a * cos_b + cos_a * sin_b, (1, reps))
    lane = lax.broadcasted_iota(jnp.int32, q.shape, 1)
    first_half = (lane % DIFF_HEAD) < (DIFF_HEAD // 2)
    gs = gs_ref[...]

    def norm_rope(t, g):
        ss = _split_dot(t * t, gs, 1)
        t = t * lax.rsqrt(ss * (1.0 / DIFF_HEAD) + RMS_EPS) * g
        rot = jnp.where(first_half,
                        pltpu.roll(t, DIFF_WIDTH - DIFF_HEAD // 2, 1),
                        pltpu.roll(t, DIFF_HEAD // 2, 1))
        return t * cos + rot * sin

    qr = norm_rope(q, qg_ref[...]) * (math.log2(math.e) / math.sqrt(DIFF_HEAD))
    kr = norm_rope(k, kg_ref[...])
    qt_ref[0] = qr.T.astype(BF16)
    k_ref[0] = kr.astype(BF16)
    vt_ref[0] = v.T.astype(BF16)


def _inproj(x, rope, g1, w_in, qg, kg, gsum, tm):
    bsz, length, _ = x.shape
    grid = (bsz, length // tm)
    cos_a, sin_a, cos_b, sin_b = rope
    return pl.pallas_call(
        _inproj_kernel,
        grid=grid,
        in_specs=[
            pl.BlockSpec((1, tm, D_MODEL), lambda b, i: (b, i, 0)),
            _const_spec((1, D_MODEL)),
            _const_spec((D_MODEL, IN_COLS)),
            pl.BlockSpec((1, 1, PAIR), lambda b, i: (i, 0, 0)),
            pl.BlockSpec((1, 1, PAIR), lambda b, i: (i, 0, 0)),
            _const_spec((tm, PAIR)),
            _const_spec((tm, PAIR)),
            _const_spec((1, DIFF_WIDTH)),
            _const_spec((1, DIFF_WIDTH)),
            _const_spec((PAIR, PAIR)),
        ],
        out_specs=[
            pl.BlockSpec((1, tm, SHIFT_COLS), lambda b, i: (b, i, 0)),
            pl.BlockSpec((1, DIFF_WIDTH, tm), lambda b, i: (b, 0, i)),
            pl.BlockSpec((1, tm, DIFF_WIDTH), lambda b, i: (b, i, 0)),
            pl.BlockSpec((1, DIFF_WIDTH, tm), lambda b, i: (b, 0, i)),
        ],
        out_shape=[
            jax.ShapeDtypeStruct((bsz, length, SHIFT_COLS), F32),
            jax.ShapeDtypeStruct((bsz, DIFF_WIDTH, length), BF16),
            jax.ShapeDtypeStruct((bsz, length, DIFF_WIDTH), BF16),
            jax.ShapeDtypeStruct((bsz, DIFF_WIDTH, length), BF16),
        ],
        compiler_params=_cparams(("parallel", "parallel")),
        name="inproj",
    )(x, g1, w_in, cos_a, sin_a, cos_b, sin_b, qg, kg, gsum)


def _prep_kernel(p_ref, ph_ref, nh_ref, p0_ref, mup_ref, mun_ref, w0_ref, w2_ref, a0_ref, a2_ref,
                 g2_ref, kk_ref, ka_ref, rk_ref, gs_ref,
                 raf_ref, kbf_ref, bktf_ref, wcf_ref, rab_ref, kbb_ref, bktb_ref, wcb_ref,
                 v_ref, g_ref, bonus_ref, *, valid_rows):
    i = pl.program_id(1)
    last = pl.num_programs(1) - 1
    p = p_ref[0]
    tm = p.shape[0]
    row = lax.broadcasted_iota(jnp.int32, p.shape, 0)
    prev_row = jnp.where(i == 0, p0_ref[7:8, :], ph_ref[0, 7:8, :])
    next_row = jnp.where(i == last, jnp.zeros_like(prev_row), nh_ref[0, 0:1, :])
    p_prev = jnp.where(row == 0, prev_row, pltpu.roll(p, 1, 0))
    p_next = jnp.where(row == tm - 1, next_row, pltpu.roll(p, tm - 1, 0))
    mu_prev, mu_next = mup_ref[...], mun_ref[...]
    z = p * (1.0 - mu_prev - mu_next) + mu_prev * p_prev + mu_next * p_next

    w = RWKV_WIDTH
    r = z[:, :w]
    k = z[:, w:2 * w]
    v = z[:, 2 * w:3 * w]
    gd = z[:, 3 * w:3 * w + GATE_LORA]
    wd = z[:, 3 * w + GATE_LORA:3 * w + GATE_LORA + LORA_PAIR]
    ad = z[:, 3 * w + GATE_LORA + LORA_PAIR:]

    wl = w0_ref[...] + _mm3(*_split2(jnp.tanh(wd)), w2_ref[0], w2_ref[1])
    lw = (-math.exp(-0.5) * math.log2(math.e)) * jax.nn.sigmoid(wl)
    al = a0_ref[...] + _mm3(*_split2(ad), a2_ref[0], a2_ref[1])
    iclr = jax.nn.sigmoid(al)
    g = _mm(jax.nn.sigmoid(gd), g2_ref[...])

    gs = gs_ref[...]
    kk = k * kk_ref[...]
    kk = kk * lax.rsqrt(jnp.maximum(_split_dot(kk * kk, gs, 1), 1e-24))
    k_a = ka_ref[...]
    a_f = iclr[:, :w]
    a_b = iclr[:, w:]
    k_f = k * (1.0 + (a_f - 1.0) * k_a)
    k_b = k * (1.0 + (a_b - 1.0) * k_a)
    bonus = _split_dot(r * k_f * rk_ref[...], gs, 1) * v
    a_neg = -kk
    b_f = kk * a_f
    b_b = kk * a_b
    lw_f = lw[:, :w]
    lw_b = lw[:, w:]

    if valid_rows is not None:
        keep = lax.broadcasted_iota(jnp.int32, r.shape, 0) < valid_rows
        zero = jnp.zeros_like(r)
        r, v, a_neg = (jnp.where(keep, t, zero) for t in (r, v, a_neg))
        k_f, k_b, b_f, b_b = (jnp.where(keep, t, zero) for t in (k_f, k_b, b_f, b_b))
        lw_f, lw_b = (jnp.where(keep, t, zero) for t in (lw_f, lw_b))

    v_ref[0] = v.astype(BF16)
    g_ref[0] = g.astype(BF16)
    bonus_ref[0] = bonus.astype(BF16)

    ti = lax.broadcasted_iota(jnp.int32, (CHUNK, CHUNK), 0)
    si = lax.broadcasted_iota(jnp.int32, (CHUNK, CHUNK), 1)
    tri_f = jnp.where(si <= ti, 1.0, 0.0).astype(BF16)
    tri_b = jnp.where(si >= ti, 1.0, 0.0).astype(BF16)

    for c in range(tm // CHUNK):
        rows = slice(c * CHUNK, (c + 1) * CHUNK)
        r_c, v_c, a_c = r[rows], v[rows], a_neg[rows]
        for (lw_d, k_d, b_d, tri, fwd, ra_ref, kb_ref, bkt_ref, wc_ref) in (
                (lw_f, k_f, b_f, tri_f, True, raf_ref, kbf_ref, bktf_ref, wcf_ref),
                (lw_b, k_b, b_b, tri_b, False, rab_ref, kbb_ref, bktb_ref, wcb_ref)):
            lw_c = lw_d[rows]
            cum = _tri_dot(tri, lw_c)
            tot = cum[CHUNK - 1:CHUNK] if fwd else cum[0:1]
            e_excl = jnp.exp2(cum - lw_c)
            e_read = jnp.exp2(cum) if fwd else e_excl
            e_inv = jnp.exp2(-cum)
            e_end = jnp.exp2(tot - cum)
            ra_ref[0, 2 * c * CHUNK:(2 * c + 1) * CHUNK] = (r_c * e_read).astype(BF16)
            ra_ref[0, (2 * c + 1) * CHUNK:(2 * c + 2) * CHUNK] = (a_c * e_excl).astype(BF16)
            kb_ref[0, 2 * c * CHUNK:(2 * c + 1) * CHUNK] = (k_d[rows] * e_inv).astype(BF16)
            kb_ref[0, (2 * c + 1) * CHUNK:(2 * c + 2) * CHUNK] = (b_d[rows] * e_inv).astype(BF16)
            bk = jnp.concatenate([b_d[rows] * e_end, k_d[rows] * e_end], axis=0)
            bkt_ref[0, c * w:(c + 1) * w] = bk.T.astype(BF16)
            wc_ref[0, 8 * c:8 * (c + 1)] = jnp.broadcast_to(jnp.exp2(tot), (8, w))


def _prep(p, prev8, consts, tm, valid_rows):
    bsz, length, _ = p.shape
    nt = length // tm
    nc = length // CHUNK
    cpt = tm // CHUNK
    hb = tm // 8
    n8 = length // 8
    mup, mun, w0, w2, a0, a2, g2, k_k, k_a, r_k, gsum = consts
    tok = lambda b, i: (b, i, 0)
    dir_specs = [
        pl.BlockSpec((1, 2 * tm, RWKV_WIDTH), tok),
        pl.BlockSpec((1, 2 * tm, RWKV_WIDTH), tok),
        pl.BlockSpec((1, cpt * RWKV_WIDTH, PAIR), tok),
        pl.BlockSpec((1, cpt * 8, RWKV_WIDTH), tok),
    ]
    dir_shapes = [
        jax.ShapeDtypeStruct((bsz, 2 * length, RWKV_WIDTH), BF16),
        jax.ShapeDtypeStruct((bsz, 2 * length, RWKV_WIDTH), BF16),
        jax.ShapeDtypeStruct((bsz, nc * RWKV_WIDTH, PAIR), BF16),
        jax.ShapeDtypeStruct((bsz, nc * 8, RWKV_WIDTH), F32),
    ]
    tok_spec = pl.BlockSpec((1, tm, RWKV_WIDTH), tok)
    tok_shape = jax.ShapeDtypeStruct((bsz, length, RWKV_WIDTH), BF16)
    return pl.pallas_call(
        functools.partial(_prep_kernel, valid_rows=valid_rows),
        grid=(bsz, nt),
        in_specs=[
            pl.BlockSpec((1, tm, SHIFT_COLS), tok),
            pl.BlockSpec((1, 8, SHIFT_COLS), lambda b, i: (b, jnp.maximum(i * hb - 1, 0), 0)),
            pl.BlockSpec((1, 8, SHIFT_COLS), lambda b, i: (b, jnp.minimum((i + 1) * hb, n8 - 1), 0)),
            _const_spec((8, SHIFT_COLS)),
            _const_spec((1, SHIFT_COLS)),
            _const_spec((1, SHIFT_COLS)),
            _const_spec((1, 2 * RWKV_WIDTH)),
            _const_spec((2, LORA_PAIR, 2 * RWKV_WIDTH)),
            _const_spec((1, 2 * RWKV_WIDTH)),
            _const_spec((2, LORA_PAIR, 2 * RWKV_WIDTH)),
            _const_spec((GATE_LORA, RWKV_WIDTH)),
            _const_spec((1, RWKV_WIDTH)),
            _const_spec((1, RWKV_WIDTH)),
            _const_spec((1, RWKV_WIDTH)),
            _const_spec((PAIR, PAIR)),
        ],
        out_specs=dir_specs + dir_specs + [tok_spec, tok_spec, tok_spec],
        out_shape=dir_shapes + dir_shapes + [tok_shape, tok_shape, tok_shape],
        compiler_params=_cparams(("parallel", "parallel")),
        name="rwkv_prep",
    )(p, p, p, prev8, mup, mun, w0, w2, a0, a2, g2, k_k, k_a, r_k, gsum)


def _row_blocks(x, lane_lo):
    zero = jnp.zeros_like(x)
    return jnp.concatenate([jnp.where(lane_lo, x, zero), jnp.where(lane_lo, zero, x)], axis=0)


def _scan_chains(dirs, cps):
    ti = lax.broadcasted_iota(jnp.int32, (CHUNK, PAIR), 0)
    li = lax.broadcasted_iota(jnp.int32, (CHUNK, PAIR), 1)
    si = li % CHUNK
    lane_lo = li < CHUNK
    lower, lower_eq, upper = ti > si, ti >= si, ti < si
    same_sub = (ti // SUB) == (si // SUB)
    eye_w = jnp.where(ti == si, 1.0, 0.0).astype(F32)
    rb = lambda t: _row_blocks(t, lane_lo)
    each = lambda fn, *cols: [fn(*vals) for vals in zip(*cols)]

    chains = [(d, j, u) for u in range(cps) for d in dirs for j in range(N_PAIRS)]
    lanes = [slice(j * PAIR, (j + 1) * PAIR) for _, j, _ in chains]
    mask_a = [lower if d[7] else upper for d, _, _ in chains]
    mask_r = [lower_eq if d[7] else upper for d, _, _ in chains]
    ra = [d[0][0, 2 * u * CHUNK:2 * (u + 1) * CHUNK, ln] for (d, _, u), ln in zip(chains, lanes)]
    kb = [d[1][0, 2 * u * CHUNK:2 * (u + 1) * CHUNK, ln] for (d, _, u), ln in zip(chains, lanes)]
    v2 = [d[4][0, u * CHUNK:(u + 1) * CHUNK, ln] for (d, _, u), ln in zip(chains, lanes)]
    bkt = [d[2][0, u * RWKV_WIDTH + j * PAIR:u * RWKV_WIDTH + (j + 1) * PAIR, :] for d, j, u in chains]
    wc = [d[3][0, 8 * u:8 * u + 1, ln] for (d, _, u), ln in zip(chains, lanes)]
    rbv = each(rb, v2)

    rows2 = lambda a, b: jnp.concatenate([a, b], axis=0)
    cols2 = lambda a, b: jnp.concatenate([a, b], axis=1)
    top, bot = (lambda t: t[:CHUNK]), (lambda t: t[CHUNK:])
    left, right = (lambda t: t[:, :PAIR]), (lambda t: t[:, PAIR:])

    akb = each(lambda a, b: _mm_nt(a, rows2(rb(b[:CHUNK]), rb(b[CHUNK:]))), ra, kb)
    a_rk = each(lambda m, t: jnp.where(m, left(top(t)), 0.0), mask_r, akb)
    a_ak = each(lambda m, t: jnp.where(m, left(bot(t)), 0.0), mask_a, akb)
    a_rb = each(lambda m, t: jnp.where(m, right(top(t)), 0.0), mask_r, akb)
    n_all = each(lambda m, t: jnp.where(m, right(bot(t)), 0.0), mask_a, akb)

    d1 = each(lambda t: jnp.where(same_sub, t, 0.0), n_all)
    e1 = each(lambda t, d: t - d, n_all, d1)
    d2 = each(lambda d: _mm(d, rb(d)), d1)
    arav = each(lambda a, b, r: _mm(rows2(a, b), r), a_rk, a_ak, rbv)
    t16 = each(lambda d: eye_w + d, d1)
    x2 = each(lambda t, d: _mm(rows2(t, d), rb(d)), t16, d2)
    t16 = each(lambda t, x: t + top(x), t16, x2)
    x4 = each(lambda t, x: _mm(rows2(t, bot(x)), rb(bot(x))), t16, x2)
    t16 = each(lambda t, x: t + top(x), t16, x4)
    t16 = each(lambda t, x: t + _mm(t, rb(bot(x))), t16, x4)
    m1 = each(lambda t, e: _mm(t, rb(e)), t16, e1)
    x6 = each(lambda m, t: _mm(m, cols2(rb(m), rb(t))), m1, t16)
    yy = each(lambda t, x: t + right(x), t16, x6)
    tinv = each(lambda y, x: y + _mm(left(x), rb(y)), yy, x6)

    tu = each(lambda t, r, a: _mm(t, cols2(rb(bot(r)), rb(bot(a).astype(BF16)))), tinv, ra, arav)
    ar = each(lambda a, t: _mm(a, cols2(rb(left(t)), rb(right(t)))), a_rb, tu)
    rp = each(lambda r, a: (top(r).astype(F32) + left(a)).astype(BF16), ra, ar)
    y0 = each(lambda a, w: right(a) + top(w), ar, arav)
    stack = each(lambda t, v: jnp.concatenate(
        [t, jnp.concatenate([jnp.zeros((CHUNK, PAIR), F32), v.astype(F32)], axis=1)], axis=0), tu, v2)
    pd = each(_mm, bkt, stack)
    heads_w = lambda t: jnp.where(lane_lo, top(t), bot(t))
    p2 = each(lambda t, w: _split2(heads_w(left(t)) + eye_w * w), pd, wc)
    d02 = each(lambda t: heads_w(right(t)), pd)

    state = {(id(d), j): d[6][j] for d in dirs for j in range(N_PAIRS)}
    for step in range(cps):
        for di, d in enumerate(dirs):
            u = step if d[7] else cps - 1 - step
            for j in range(N_PAIRS):
                c = (u * len(dirs) + di) * N_PAIRS + j
                s_hi, s_lo = _split2(state[(id(d), j)])
                p_hi, p_lo = p2[c]
                res = jnp.dot(jnp.concatenate([rp[c], p_hi, p_lo], axis=0), rb(s_hi), preferred_element_type=F32)
                d[5][0, u * CHUNK:(u + 1) * CHUNK, lanes[c]] = res[:CHUNK] + y0[c]
                state[(id(d), j)] = (res[CHUNK:2 * CHUNK] + res[2 * CHUNK:]
                                     + jnp.dot(p_hi, rb(s_lo), preferred_element_type=F32) + d02[c])
    for d in dirs:
        for j in range(N_PAIRS):
            d[6][j] = state[(id(d), j)]


def _scan_kernel(raf_ref, kbf_ref, bktf_ref, wcf_ref, vf_ref,
                 rab_ref, kbb_ref, bktb_ref, wcb_ref, vb_ref, s0_ref,
                 yf_ref, yb_ref, sout_ref, sf_ref, sb_ref, *, cps):
    i = pl.program_id(1)

    @pl.when(i == 0)
    def _():
        sf_ref[...] = s0_ref[0]
        sb_ref[...] = jnp.zeros_like(sb_ref)

    _scan_chains(((raf_ref, kbf_ref, bktf_ref, wcf_ref, vf_ref, yf_ref, sf_ref, True),
                  (rab_ref, kbb_ref, bktb_ref, wcb_ref, vb_ref, yb_ref, sb_ref, False)), cps)

    @pl.when(i == pl.num_programs(1) - 1)
    def _():
        sout_ref[0] = sf_ref[...]


def _scan(prep_out, s0, cps):
    raf, kbf, bktf, wcf, rab, kbb, bktb, wcb, v = prep_out[:9]
    bsz, length, _ = v.shape
    steps = length // (cps * CHUNK)
    fw = lambda b, i: (b, i, 0)
    bw = lambda b, i: (b, steps - 1 - i, 0)

    def dir_specs(im):
        return [
            pl.BlockSpec((1, 2 * cps * CHUNK, RWKV_WIDTH), im),
            pl.BlockSpec((1, 2 * cps * CHUNK, RWKV_WIDTH), im),
            pl.BlockSpec((1, cps * RWKV_WIDTH, PAIR), im),
            pl.BlockSpec((1, cps * 8, RWKV_WIDTH), im),
            pl.BlockSpec((1, cps * CHUNK, RWKV_WIDTH), im),
        ]

    state_spec = pl.BlockSpec((1, N_PAIRS, RWKV_HEAD, PAIR), lambda b, i: (b, 0, 0, 0))
    return pl.pallas_call(
        functools.partial(_scan_kernel, cps=cps),
        grid=(bsz, steps),
        in_specs=dir_specs(fw) + dir_specs(bw) + [state_spec],
        out_specs=[
            pl.BlockSpec((1, cps * CHUNK, RWKV_WIDTH), fw),
            pl.BlockSpec((1, cps * CHUNK, RWKV_WIDTH), bw),
            state_spec,
        ],
        out_shape=[
            jax.ShapeDtypeStruct((bsz, length, RWKV_WIDTH), F32),
            jax.ShapeDtypeStruct((bsz, length, RWKV_WIDTH), F32),
            jax.ShapeDtypeStruct((bsz, N_PAIRS, RWKV_HEAD, PAIR), F32),
        ],
        scratch_shapes=[pltpu.VMEM((N_PAIRS, RWKV_HEAD, PAIR), F32)] * 2,
        compiler_params=_cparams(("parallel", "arbitrary")),
        name="rwkv_scan",
    )(raf, kbf, bktf, wcf, v, rab, kbb, bktb, wcb, v, s0)


def _attn_kernel(k_ref, qt_ref, vt_ref, km_ref, vtm_ref, lq1_ref, lk1_ref, lq2_ref, lk2_ref, sg_ref,
                 o_ref, m_ref, acc_ref, *, lam_init):
    kv = pl.program_id(2)
    tk = k_ref.shape[1]
    tq = qt_ref.shape[2]
    map_row = lax.broadcasted_iota(jnp.int32, (PAIR, tq), 0) < DIFF_HEAD

    def q_weights(hc):
        qt2 = qt_ref[0, (hc // 2) * PAIR:(hc // 2 + 1) * PAIR, :]
        keep = map_row if hc % 2 == 0 else jnp.logical_not(map_row)
        return jnp.where(keep, qt2, jnp.zeros_like(qt2))

    def key_max(s):
        rows = min(MAX_ROWS, s.shape[0])
        part = jnp.max(s.reshape(s.shape[0] // rows, rows, s.shape[1]), axis=0)
        return jnp.max(part, axis=0, keepdims=True)

    def values_and_ones(vt_h):
        return jnp.concatenate([vt_h, jnp.ones((ONES_ROWS, vt_h.shape[1]), BF16)], axis=0)

    @pl.when(kv == 0)
    def _():
        heads = [slice((hc // 2) * PAIR, (hc // 2 + 1) * PAIR) for hc in range(N_MAPS)]
        s_meta = [jnp.dot(km_ref[0, :, heads[hc]], q_weights(hc), preferred_element_type=F32)
                  for hc in range(N_MAPS)]
        m_meta = [key_max(s) for s in s_meta]
        p_meta = [jnp.exp2(s - m).astype(BF16) for s, m in zip(s_meta, m_meta)]
        for hc in range(N_MAPS):
            acc_ref[hc] = jnp.dot(values_and_ones(vtm_ref[0, heads[hc], :]), p_meta[hc],
                                  preferred_element_type=F32)
            m_ref[hc] = jnp.broadcast_to(m_meta[hc], (8, tq))

    def score(hc):
        head = slice((hc // 2) * PAIR, (hc // 2 + 1) * PAIR)
        return jnp.dot(k_ref[0, :, head], q_weights(hc), preferred_element_type=F32)

    scores = {hc: score(hc) for hc in range(SCORES_AHEAD)}
    for hc in range(N_MAPS):
        if hc + SCORES_AHEAD < N_MAPS:
            scores[hc + SCORES_AHEAD] = score(hc + SCORES_AHEAD)
        s = scores.pop(hc)
        head = slice((hc // 2) * PAIR, (hc // 2 + 1) * PAIR)
        m_prev = m_ref[hc]
        m_next = jnp.maximum(m_prev, key_max(s))
        alpha = jnp.exp2(m_prev - m_next)
        p = jnp.exp2(s - m_next[0:1])
        acc_ref[hc] = alpha[0:1] * acc_ref[hc] + jnp.dot(values_and_ones(vt_ref[0, head, :]), p.astype(BF16),
                                                         preferred_element_type=F32)
        m_ref[hc] = m_next

    @pl.when(kv == pl.num_programs(2) - 1)
    def _():
        lam = (jnp.exp(jnp.sum(lq1_ref[...] * lk1_ref[...], axis=-1, keepdims=True))
               - jnp.exp(jnp.sum(lq2_ref[...] * lk2_ref[...], axis=-1, keepdims=True)) + lam_init)
        gain = jnp.tile(sg_ref[...], (1, tq // PAIR)) * (1.0 - lam_init)
        for hd in range(DIFF_HEADS):
            a1 = acc_ref[2 * hd]
            a2 = acc_ref[2 * hd + 1]
            o = a1[:PAIR] / a1[PAIR:PAIR + 1] - lam * (a2[:PAIR] / a2[PAIR:PAIR + 1])
            ms = jnp.mean(o * o, axis=0, keepdims=True)
            o = o * lax.rsqrt(ms + RMS_EPS) * gain
            o_ref[0, :, hd * PAIR:(hd + 1) * PAIR] = o.T.astype(BF16)


def _attention(k, qt, vt, km, vtm, lam_params, subln_g, lam_init, tq, tk):
    bsz, length, _ = k.shape
    lq1, lk1, lq2, lk2 = lam_params
    return pl.pallas_call(
        functools.partial(_attn_kernel, lam_init=lam_init),
        grid=(bsz, length // tq, length // tk),
        in_specs=[
            pl.BlockSpec((1, tk, DIFF_WIDTH), lambda b, i, j: (b, j, 0)),
            pl.BlockSpec((1, DIFF_WIDTH, tq), lambda b, i, j: (b, 0, i)),
            pl.BlockSpec((1, DIFF_WIDTH, tk), lambda b, i, j: (b, 0, j)),
            _const_spec((1, N_META, DIFF_WIDTH)),
            _const_spec((1, DIFF_WIDTH, N_META)),
            _const_spec((1, DIFF_HEAD)),
            _const_spec((1, DIFF_HEAD)),
            _const_spec((1, DIFF_HEAD)),
            _const_spec((1, DIFF_HEAD)),
            _const_spec((PAIR, PAIR)),
        ],
        out_specs=pl.BlockSpec((1, tq, DIFF_WIDTH), lambda b, i, j: (b, i, 0)),
        out_shape=jax.ShapeDtypeStruct((bsz, length, DIFF_WIDTH), BF16),
        scratch_shapes=[pltpu.VMEM((N_MAPS, 8, tq), F32), pltpu.VMEM((N_MAPS, PAIR + ONES_ROWS, tq), F32)],
        compiler_params=_cparams(("parallel", "parallel", "arbitrary")),
        name="diff_attn",
    )(k, qt, vt, km, vtm, lq1, lk1, lq2, lk2, subln_g)


def _out_kernel(x_ref, yf_ref, yb_ref, bonus_ref, g_ref, od_ref, lnw_ref, lnb_ref, gs_ref,
                wo_ref, g2n_ref, wg_ref, wu_ref, wd_ref, o_ref):
    gs = gs_ref[...]
    y = yf_ref[0] + yb_ref[0]
    mu = _split_dot(y, gs, 2) * (1.0 / RWKV_HEAD)
    yc = y - mu
    var = _split_dot(yc * yc, gs, 2) * (1.0 / RWKV_HEAD)
    yn = yc * lax.rsqrt(var + GN_EPS) * lnw_ref[...] + lnb_ref[...]
    o_rwkv = (yn + bonus_ref[0].astype(F32)) * g_ref[0].astype(F32)
    x1 = (x_ref[0]
          + jnp.dot(o_rwkv.astype(BF16), wo_ref[:RWKV_WIDTH, :], preferred_element_type=F32)
          + jnp.dot(od_ref[0], wo_ref[RWKV_WIDTH:, :], preferred_element_type=F32))
    ms = jnp.mean(x1 * x1, axis=-1, keepdims=True)
    h2 = (x1 * lax.rsqrt(ms + RMS_EPS) * g2n_ref[...]).astype(BF16)
    gate = jnp.dot(h2, wg_ref[...], preferred_element_type=F32)
    up = jnp.dot(h2, wu_ref[...], preferred_element_type=F32)
    act = (gate * jax.nn.sigmoid(gate) * up).astype(BF16)
    o_ref[0] = x1 + jnp.dot(act, wd_ref[...], preferred_element_type=F32)


def _out(x, yf, yb, bonus, g, od, consts, tm):
    bsz, length, _ = x.shape
    lnw, lnb, gsum, wo, g2n, wg, wu, wd = consts
    tok = lambda b, i: (b, i, 0)
    half = pl.BlockSpec((1, tm, RWKV_WIDTH), tok)
    return pl.pallas_call(
        _out_kernel,
        grid=(bsz, length // tm),
        in_specs=[
            pl.BlockSpec((1, tm, D_MODEL), tok), half, half, half, half, half,
            _const_spec((1, RWKV_WIDTH)),
            _const_spec((1, RWKV_WIDTH)),
            _const_spec((PAIR, PAIR)),
            _const_spec((D_MODEL, D_MODEL)),
            _const_spec((1, D_MODEL)),
            _const_spec((D_MODEL, D_FF)),
            _const_spec((D_MODEL, D_FF)),
            _const_spec((D_FF, D_MODEL)),
        ],
        out_specs=pl.BlockSpec((1, tm, D_MODEL), tok),
        out_shape=jax.ShapeDtypeStruct((bsz, length, D_MODEL), F32),
        compiler_params=_cparams(("parallel", "parallel")),
        name="out_ffn",
    )(x, yf, yb, bonus, g, od, lnw, lnb, gsum, wo, g2n, wg, wu, wd)


def _rope_tables(length, offset, tm):
    inv = ROPE_THETA ** (-jnp.arange(0, DIFF_HEAD, 2, dtype=F32) / DIFF_HEAD)

    def tables(pos):
        ang = pos[:, None] * inv[None, :]
        sin = jnp.sin(ang)
        return jnp.concatenate([jnp.cos(ang)] * 4, axis=-1), jnp.concatenate([-sin, sin, -sin, sin], axis=-1)

    cos_a, sin_a = tables(jnp.arange(0, length, tm, dtype=F32) + offset)
    cos_b, sin_b = tables(jnp.arange(tm, dtype=F32))
    return cos_a[:, None, :], sin_a[:, None, :], cos_b, sin_b


def _block_diag2(a, b):
    za = jnp.zeros_like(a)
    return jnp.concatenate([jnp.concatenate([a, za], axis=1), jnp.concatenate([za, b], axis=1)], axis=0)


def _tile_for(length, pref):
    t = pref
    while length % t:
        t //= 2
    return t


def kernel(x_prompt, x_sample, meta_tokens, norm1_g, w_in, shift_mu_prev, shift_mu_next, w0_f, w2_f, w0_b, w2_b, a0_f, a2_f, a0_b, a2_b, g2, k_k, k_a, r_k, ln_x_w, ln_x_b, q_norm_g, k_norm_g, lam_q1, lam_k1, lam_q2, lam_k2, subln_g, w_out, norm2_g, w_gate, w_up, w_down):
    lam_init = 0.8 - 0.6 * math.exp(-0.3 * 0)
    row = lambda t: t.reshape(1, -1).astype(F32)
    gi = jnp.arange(PAIR) // RWKV_HEAD
    gsum = (gi[:, None] == gi[None, :]).astype(BF16)

    g1 = row(norm1_g[0])
    w_in_b = w_in[0].astype(BF16)
    qg = row(jnp.tile(q_norm_g[0], N_MAPS))
    kg = row(jnp.tile(k_norm_g[0], N_MAPS))
    prep_consts = (
        row(shift_mu_prev[0]), row(shift_mu_next[0]),
        row(jnp.concatenate([w0_f[0], w0_b[0]])), jnp.stack(_split2(_block_diag2(w2_f[0], w2_b[0]).astype(F32))),
        row(jnp.concatenate([a0_f[0], a0_b[0]])), jnp.stack(_split2(_block_diag2(a2_f[0], a2_b[0]).astype(F32))),
        g2[0].astype(BF16), row(k_k[0]), row(k_a[0]), row(r_k[0]), gsum,
    )
    lam_params = (row(lam_q1[0]), row(lam_k1[0]), row(lam_q2[0]), row(lam_k2[0]))
    sg = jnp.broadcast_to(subln_g[0].astype(F32)[:, None], (PAIR, PAIR))
    out_consts = (row(ln_x_w[0]), row(ln_x_b[0]), gsum, w_out[0].astype(BF16), row(norm2_g[0]),
                  w_gate[0].astype(BF16), w_up[0].astype(BF16), w_down[0].astype(BF16))

    meta_x = jnp.zeros((1, META_PAD, D_MODEL), F32).at[0, :N_META].set(meta_tokens.astype(F32))
    p_meta, _, k_meta, vt_meta = _inproj(meta_x, _rope_tables(META_PAD, 0, META_PAD), g1, w_in_b, qg, kg, gsum,
                                         META_PAD)
    prev8 = p_meta[0, N_META - 8:N_META]
    zero8 = jnp.zeros_like(prev8)

    def group(x):
        bsz, length, _ = x.shape
        tm = _tile_for(length, 512)
        p, qt, k, vt = _inproj(x, _rope_tables(length, N_META, tm), g1, w_in_b, qg, kg, gsum, tm)
        pm = jnp.concatenate(
            [jnp.broadcast_to(p_meta[:, :N_META], (bsz, N_META, SHIFT_COLS)),
             p[:, :CHUNK - N_META]], axis=1)
        meta_prep = _prep(pm, zero8, prep_consts, CHUNK, N_META)
        zero_state = jnp.zeros((bsz, N_PAIRS, RWKV_HEAD, PAIR), F32)
        s_meta = _scan(meta_prep, zero_state, 1)[2]
        prep_out = _prep(p, prev8, prep_consts, _tile_for(length, 256), None)
        yf, yb, _ = _scan(prep_out, s_meta, 4)
        od = _attention(k, qt, vt, k_meta[:, :N_META], vt_meta[:, :, :N_META], lam_params, sg, lam_init,
                        _tile_for(length, 512), _tile_for(length, 1024))
        return _out(x, yf, yb, prep_out[10], prep_out[9], od, out_consts, _tile_for(length, 512))

    return (group(x_prompt), group(x_sample))
```

```python
import functools
import math

import jax
import jax.numpy as jnp
from jax import lax
from jax.experimental import pallas as pl
from jax.experimental.pallas import tpu as pltpu

F32 = jnp.float32
BF16 = jnp.bfloat16

D_MODEL = 1024
N_META = 16
RWKV_WIDTH = 512
RWKV_HEAD = 64
DIFF_WIDTH = 512
DIFF_HEAD = 64
DIFF_HEADS = 4
N_MAPS = 2 * DIFF_HEADS
GATE_LORA = 128
LORA_PAIR = 128
SHIFT_COLS = 3 * RWKV_WIDTH + GATE_LORA + 2 * LORA_PAIR
IN_COLS = SHIFT_COLS + 3 * DIFF_WIDTH
D_FF = 2816
ROPE_THETA = 10000.0
RMS_EPS = 1e-6
GN_EPS = 64e-5
CHUNK = 64
SUB = 16
PAIR = 128
N_PAIRS = RWKV_WIDTH // PAIR
META_PAD = 128
SCORES_AHEAD = 2
ONES_ROWS = 16
MAX_ROWS = 64
VMEM_LIMIT_BYTES = 56 * 1024 * 1024


def _cparams(sem):
    return pltpu.CompilerParams(dimension_semantics=sem, vmem_limit_bytes=VMEM_LIMIT_BYTES)


def _const_spec(shape):
    nd = len(shape)
    return pl.BlockSpec(shape, lambda *_: (0,) * nd, pipeline_mode=pl.Buffered(1))


def _mm(a, b):
    return jnp.dot(a.astype(BF16), b.astype(BF16), preferred_element_type=F32)


def _mm_nt(a, b):
    return lax.dot_general(a.astype(BF16), b.astype(BF16), (((1,), (1,)), ((), ())),
                           preferred_element_type=F32)


def _split_dot(x, m, terms):
    outs = []
    for c in range(x.shape[1] // PAIR):
        acc = None
        rem = x[:, c * PAIR:(c + 1) * PAIR]
        for t in range(terms):
            piece = rem.astype(BF16)
            part = jnp.dot(piece, m, preferred_element_type=F32)
            acc = part if acc is None else acc + part
            if t + 1 < terms:
                rem = rem - piece.astype(F32)
        outs.append(acc)
    return jnp.concatenate(outs, axis=1)


def _split2(x):
    hi = x.astype(BF16)
    return hi, (x - hi.astype(F32)).astype(BF16)


def _mm3(a_hi, a_lo, b_hi, b_lo):
    dot = functools.partial(jnp.dot, preferred_element_type=F32)
    return dot(a_hi, b_hi) + (dot(a_lo, b_hi) + dot(a_hi, b_lo))


def _tri_dot(tri, x):
    hi, lo = _split2(x)
    return jnp.dot(tri, hi, preferred_element_type=F32) + jnp.dot(tri, lo, preferred_element_type=F32)


def _inproj_kernel(x_ref, g1_ref, w_ref, cos_a_ref, sin_a_ref, cos_b_ref, sin_b_ref, qg_ref, kg_ref, gs_ref,
                   p_ref, qt_ref, k_ref, vt_ref):
    x = x_ref[0]
    ms = jnp.mean(x * x, axis=-1, keepdims=True)
    h = (x * lax.rsqrt(ms + RMS_EPS) * g1_ref[...]).astype(BF16)
    p_ref[0] = jnp.dot(h, w_ref[:, :SHIFT_COLS], preferred_element_type=F32)
    qkv = jnp.dot(h, w_ref[:, SHIFT_COLS:], preferred_element_type=F32)
    q = qkv[:, :DIFF_WIDTH]
    k = qkv[:, DIFF_WIDTH:2 * DIFF_WIDTH]
    v = qkv[:, 2 * DIFF_WIDTH:]

    cos_a, sin_a, cos_b, sin_b = cos_a_ref[0], sin_a_ref[0], cos_b_ref[...], sin_b_ref[...]
    reps = DIFF_WIDTH // cos_b.shape[1]
    cos = jnp.tile(cos_a * cos_b - sin_a * sin_b, (1, reps))
    sin = jnp.tile(sin_a * cos_b + cos_a * sin_b, (1, reps))
    lane = lax.broadcasted_iota(jnp.int32, q.shape, 1)
    first_half = (lane % DIFF_HEAD) < (DIFF_HEAD // 2)
    gs = gs_ref[...]

    def norm_rope(t, g):
        ss = _split_dot(t * t, gs, 1)
        t = t * lax.rsqrt(ss * (1.0 / DIFF_HEAD) + RMS_EPS) * g
        rot = jnp.where(first_half,
                        pltpu.roll(t, DIFF_WIDTH - DIFF_HEAD // 2, 1),
                        pltpu.roll(t, DIFF_HEAD // 2, 1))
        return t * cos + rot * sin

    qr = norm_rope(q, qg_ref[...]) * (math.log2(math.e) / math.sqrt(DIFF_HEAD))
    kr = norm_rope(k, kg_ref[...])
    qt_ref[0] = qr.T.astype(BF16)
    k_ref[0] = kr.astype(BF16)
    vt_ref[0] = v.T.astype(BF16)


def _inproj(x, rope, g1, w_in, qg, kg, gsum, tm):
    bsz, length, _ = x.shape
    grid = (bsz, length // tm)
    cos_a, sin_a, cos_b, sin_b = rope
    return pl.pallas_call(
        _inproj_kernel,
        grid=grid,
        in_specs=[
            pl.BlockSpec((1, tm, D_MODEL), lambda b, i: (b, i, 0)),
            _const_spec((1, D_MODEL)),
            _const_spec((D_MODEL, IN_COLS)),
            pl.BlockSpec((1, 1, PAIR), lambda b, i: (i, 0, 0)),
            pl.BlockSpec((1, 1, PAIR), lambda b, i: (i, 0, 0)),
            _const_spec((tm, PAIR)),
            _const_spec((tm, PAIR)),
            _const_spec((1, DIFF_WIDTH)),
            _const_spec((1, DIFF_WIDTH)),
            _const_spec((PAIR, PAIR)),
        ],
        out_specs=[
            pl.BlockSpec((1, tm, SHIFT_COLS), lambda b, i: (b, i, 0)),
            pl.BlockSpec((1, DIFF_WIDTH, tm), lambda b, i: (b, 0, i)),
            pl.BlockSpec((1, tm, DIFF_WIDTH), lambda b, i: (b, i, 0)),
            pl.BlockSpec((1, DIFF_WIDTH, tm), lambda b, i: (b, 0, i)),
        ],
        out_shape=[
            jax.ShapeDtypeStruct((bsz, length, SHIFT_COLS), F32),
            jax.ShapeDtypeStruct((bsz, DIFF_WIDTH, length), BF16),
            jax.ShapeDtypeStruct((bsz, length, DIFF_WIDTH), BF16),
            jax.ShapeDtypeStruct((bsz, DIFF_WIDTH, length), BF16),
        ],
        compiler_params=_cparams(("parallel", "parallel")),
        name="inproj",
    )(x, g1, w_in, cos_a, sin_a, cos_b, sin_b, qg, kg, gsum)


def _prep_kernel(p_ref, ph_ref, nh_ref, p0_ref, mup_ref, mun_ref, w0_ref, w2_ref, a0_ref, a2_ref,
                 g2_ref, kk_ref, ka_ref, rk_ref, gs_ref,
                 raf_ref, kbf_ref, bktf_ref, wcf_ref, rab_ref, kbb_ref, bktb_ref, wcb_ref,
                 v_ref, g_ref, bonus_ref, *, valid_rows):
    i = pl.program_id(1)
    last = pl.num_programs(1) - 1
    p = p_ref[0]
    tm = p.shape[0]
    row = lax.broadcasted_iota(jnp.int32, p.shape, 0)
    prev_row = jnp.where(i == 0, p0_ref[7:8, :], ph_ref[0, 7:8, :])
    next_row = jnp.where(i == last, jnp.zeros_like(prev_row), nh_ref[0, 0:1, :])
    p_prev = jnp.where(row == 0, prev_row, pltpu.roll(p, 1, 0))
    p_next = jnp.where(row == tm - 1, next_row, pltpu.roll(p, tm - 1, 0))
    mu_prev, mu_next = mup_ref[...], mun_ref[...]
    z = p * (1.0 - mu_prev - mu_next) + mu_prev * p_prev + mu_next * p_next

    w = RWKV_WIDTH
    r = z[:, :w]
    k = z[:, w:2 * w]
    v = z[:, 2 * w:3 * w]
    gd = z[:, 3 * w:3 * w + GATE_LORA]
    wd = z[:, 3 * w + GATE_LORA:3 * w + GATE_LORA + LORA_PAIR]
    ad = z[:, 3 * w + GATE_LORA + LORA_PAIR:]

    wl = w0_ref[...] + _mm3(*_split2(jnp.tanh(wd)), w2_ref[0], w2_ref[1])
    lw = (-math.exp(-0.5) * math.log2(math.e)) * jax.nn.sigmoid(wl)
    al = a0_ref[...] + _mm3(*_split2(ad), a2_ref[0], a2_ref[1])
    iclr = jax.nn.sigmoid(al)
    g = _mm(jax.nn.sigmoid(gd), g2_ref[...])

    gs = gs_ref[...]
    kk = k * kk_ref[...]
    kk = kk * lax.rsqrt(jnp.maximum(_split_dot(kk * kk, gs, 1), 1e-24))
    k_a = ka_ref[...]
    a_f = iclr[:, :w]
    a_b = iclr[:, w:]
    k_f = k * (1.0 + (a_f - 1.0) * k_a)
    k_b = k * (1.0 + (a_b - 1.0) * k_a)
    bonus = _split_dot(r * k_f * rk_ref[...], gs, 1) * v
    a_neg = -kk
    b_f = kk * a_f
    b_b = kk * a_b
    lw_f = lw[:, :w]
    lw_b = lw[:, w:]

    if valid_rows is not None:
        keep = lax.broadcasted_iota(jnp.int32, r.shape, 0) < valid_rows
        zero = jnp.zeros_like(r)
        r, v, a_neg = (jnp.where(keep, t, zero) for t in (r, v, a_neg))
        k_f, k_b, b_f, b_b = (jnp.where(keep, t, zero) for t in (k_f, k_b, b_f, b_b))
        lw_f, lw_b = (jnp.where(keep, t, zero) for t in (lw_f, lw_b))

    v_ref[0] = v.astype(BF16)
    g_ref[0] = g.astype(BF16)
    bonus_ref[0] = bonus.astype(BF16)

    ti = lax.broadcasted_iota(jnp.int32, (CHUNK, CHUNK), 0)
    si = lax.broadcasted_iota(jnp.int32, (CHUNK, CHUNK), 1)
    tri_f = jnp.where(si <= ti, 1.0, 0.0).astype(BF16)
    tri_b = jnp.where(si >= ti, 1.0, 0.0).astype(BF16)

    for c in range(tm // CHUNK):
        rows = slice(c * CHUNK, (c + 1) * CHUNK)
        r_c, v_c, a_c = r[rows], v[rows], a_neg[rows]
        for (lw_d, k_d, b_d, tri, fwd, ra_ref, kb_ref, bkt_ref, wc_ref) in (
                (lw_f, k_f, b_f, tri_f, True, raf_ref, kbf_ref, bktf_ref, wcf_ref),
                (lw_b, k_b, b_b, tri_b, False, rab_ref, kbb_ref, bktb_ref, wcb_ref)):
            lw_c = lw_d[rows]
            cum = _tri_dot(tri, lw_c)
            tot = cum[CHUNK - 1:CHUNK] if fwd else cum[0:1]
            e_excl = jnp.exp2(cum - lw_c)
            e_read = jnp.exp2(cum) if fwd else e_excl
            e_inv = jnp.exp2(-cum)
            e_end = jnp.exp2(tot - cum)
            ra_ref[0, 2 * c * CHUNK:(2 * c + 1) * CHUNK] = (r_c * e_read).astype(BF16)
            ra_ref[0, (2 * c + 1) * CHUNK:(2 * c + 2) * CHUNK] = (a_c * e_excl).astype(BF16)
            kb_ref[0, 2 * c * CHUNK:(2 * c + 1) * CHUNK] = (k_d[rows] * e_inv).astype(BF16)
            kb_ref[0, (2 * c + 1) * CHUNK:(2 * c + 2) * CHUNK] = (b_d[rows] * e_inv).astype(BF16)
            bk = jnp.concatenate([b_d[rows] * e_end, k_d[rows] * e_end], axis=0)
            bkt_ref[0, c * w:(c + 1) * w] = bk.T.astype(BF16)
            wc_ref[0, 8 * c:8 * (c + 1)] = jnp.broadcast_to(jnp.exp2(tot), (8, w))


def _prep(p, prev8, consts, tm, valid_rows):
    bsz, length, _ = p.shape
    nt = length // tm
    nc = length // CHUNK
    cpt = tm // CHUNK
    hb = tm // 8
    n8 = length // 8
    mup, mun, w0, w2, a0, a2, g2, k_k, k_a, r_k, gsum = consts
    tok = lambda b, i: (b, i, 0)
    dir_specs = [
        pl.BlockSpec((1, 2 * tm, RWKV_WIDTH), tok),
        pl.BlockSpec((1, 2 * tm, RWKV_WIDTH), tok),
        pl.BlockSpec((1, cpt * RWKV_WIDTH, PAIR), tok),
        pl.BlockSpec((1, cpt * 8, RWKV_WIDTH), tok),
    ]
    dir_shapes = [
        jax.ShapeDtypeStruct((bsz, 2 * length, RWKV_WIDTH), BF16),
        jax.ShapeDtypeStruct((bsz, 2 * length, RWKV_WIDTH), BF16),
        jax.ShapeDtypeStruct((bsz, nc * RWKV_WIDTH, PAIR), BF16),
        jax.ShapeDtypeStruct((bsz, nc * 8, RWKV_WIDTH), F32),
    ]
    tok_spec = pl.BlockSpec((1, tm, RWKV_WIDTH), tok)
    tok_shape = jax.ShapeDtypeStruct((bsz, length, RWKV_WIDTH), BF16)
    return pl.pallas_call(
        functools.partial(_prep_kernel, valid_rows=valid_rows),
        grid=(bsz, nt),
        in_specs=[
            pl.BlockSpec((1, tm, SHIFT_COLS), tok),
            pl.BlockSpec((1, 8, SHIFT_COLS), lambda b, i: (b, jnp.maximum(i * hb - 1, 0), 0)),
            pl.BlockSpec((1, 8, SHIFT_COLS), lambda b, i: (b, jnp.minimum((i + 1) * hb, n8 - 1), 0)),
            _const_spec((8, SHIFT_COLS)),
            _const_spec((1, SHIFT_COLS)),
            _const_spec((1, SHIFT_COLS)),
            _const_spec((1, 2 * RWKV_WIDTH)),
            _const_spec((2, LORA_PAIR, 2 * RWKV_WIDTH)),
            _const_spec((1, 2 * RWKV_WIDTH)),
            _const_spec((2, LORA_PAIR, 2 * RWKV_WIDTH)),
            _const_spec((GATE_LORA, RWKV_WIDTH)),
            _const_spec((1, RWKV_WIDTH)),
            _const_spec((1, RWKV_WIDTH)),
            _const_spec((1, RWKV_WIDTH)),
            _const_spec((PAIR, PAIR)),
        ],
        out_specs=dir_specs + dir_specs + [tok_spec, tok_spec, tok_spec],
        out_shape=dir_shapes + dir_shapes + [tok_shape, tok_shape, tok_shape],
        compiler_params=_cparams(("parallel", "parallel")),
        name="rwkv_prep",
    )(p, p, p, prev8, mup, mun, w0, w2, a0, a2, g2, k_k, k_a, r_k, gsum)


def _row_blocks(x, lane_lo):
    zero = jnp.zeros_like(x)
    return jnp.concatenate([jnp.where(lane_lo, x, zero), jnp.where(lane_lo, zero, x)], axis=0)


def _scan_chains(dirs, cps):
    ti = lax.broadcasted_iota(jnp.int32, (CHUNK, PAIR), 0)
    li = lax.broadcasted_iota(jnp.int32, (CHUNK, PAIR), 1)
    si = li % CHUNK
    lane_lo = li < CHUNK
    lower, lower_eq, upper = ti > si, ti >= si, ti < si
    same_sub = (ti // SUB) == (si // SUB)
    eye_w = jnp.where(ti == si, 1.0, 0.0).astype(F32)
    rb = lambda t: _row_blocks(t, lane_lo)
    each = lambda fn, *cols: [fn(*vals) for vals in zip(*cols)]

    chains = [(d, j, u) for u in range(cps) for d in dirs for j in range(N_PAIRS)]
    lanes = [slice(j * PAIR, (j + 1) * PAIR) for _, j, _ in chains]
    mask_a = [lower if d[7] else upper for d, _, _ in chains]
    mask_r = [lower_eq if d[7] else upper for d, _, _ in chains]
    ra = [d[0][0, 2 * u * CHUNK:2 * (u + 1) * CHUNK, ln] for (d, _, u), ln in zip(chains, lanes)]
    kb = [d[1][0, 2 * u * CHUNK:2 * (u + 1) * CHUNK, ln] for (d, _, u), ln in zip(chains, lanes)]
    v2 = [d[4][0, u * CHUNK:(u + 1) * CHUNK, ln] for (d, _, u), ln in zip(chains, lanes)]
    bkt = [d[2][0, u * RWKV_WIDTH + j * PAIR:u * RWKV_WIDTH + (j + 1) * PAIR, :] for d, j, u in chains]
    wc = [d[3][0, 8 * u:8 * u + 1, ln] for (d, _, u), ln in zip(chains, lanes)]
    rbv = each(rb, v2)

    rows2 = lambda a, b: jnp.concatenate([a, b], axis=0)
    cols2 = lambda a, b: jnp.concatenate([a, b], axis=1)
    top, bot = (lambda t: t[:CHUNK]), (lambda t: t[CHUNK:])
    left, right = (lambda t: t[:, :PAIR]), (lambda t: t[:, PAIR:])

    akb = each(lambda a, b: _mm_nt(a, rows2(rb(b[:CHUNK]), rb(b[CHUNK:]))), ra, kb)
    a_rk = each(lambda m, t: jnp.where(m, left(top(t)), 0.0), mask_r, akb)
    a_ak = each(lambda m, t: jnp.where(m, left(bot(t)), 0.0), mask_a, akb)
    a_rb = each(lambda m, t: jnp.where(m, right(top(t)), 0.0), mask_r, akb)
    n_all = each(lambda m, t: jnp.where(m, right(bot(t)), 0.0), mask_a, akb)

    d1 = each(lambda t: jnp.where(same_sub, t, 0.0), n_all)
    e1 = each(lambda t, d: t - d, n_all, d1)
    d2 = each(lambda d: _mm(d, rb(d)), d1)
    arav = each(lambda a, b, r: _mm(rows2(a, b), r), a_rk, a_ak, rbv)
    t16 = each(lambda d: eye_w + d, d1)
    x2 = each(lambda t, d: _mm(rows2(t, d), rb(d)), t16, d2)
    t16 = each(lambda t, x: t + top(x), t16, x2)
    x4 = each(lambda t, x: _mm(rows2(t, bot(x)), rb(bot(x))), t16, x2)
    t16 = each(lambda t, x: t + top(x), t16, x4)
    t16 = each(lambda t, x: t + _mm(t, rb(bot(x))), t16, x4)
    m1 = each(lambda t, e: _mm(t, rb(e)), t16, e1)
    x6 = each(lambda m, t: _mm(m, cols2(rb(m), rb(t))), m1, t16)
    yy = each(lambda t, x: t + right(x), t16, x6)
    tinv = each(lambda y, x: y + _mm(left(x), rb(y)), yy, x6)

    tu = each(lambda t, r, a: _mm(t, cols2(rb(bot(r)), rb(bot(a).astype(BF16)))), tinv, ra, arav)
    ar = each(lambda a, t: _mm(a, cols2(rb(left(t)), rb(right(t)))), a_rb, tu)
    rp = each(lambda r, a: (top(r).astype(F32) + left(a)).astype(BF16), ra, ar)
    y0 = each(lambda a, w: right(a) + top(w), ar, arav)
    stack = each(lambda t, v: jnp.concatenate(
        [t, jnp.concatenate([jnp.zeros((CHUNK, PAIR), F32), v.astype(F32)], axis=1)], axis=0), tu, v2)
    pd = each(_mm, bkt, stack)
    heads_w = lambda t: jnp.where(lane_lo, top(t), bot(t))
    p2 = each(lambda t, w: _split2(heads_w(left(t)) + eye_w * w), pd, wc)
    d02 = each(lambda t: heads_w(right(t)), pd)

    state = {(id(d), j): d[6][j] for d in dirs for j in range(N_PAIRS)}
    for step in range(cps):
        for di, d in enumerate(dirs):
            u = step if d[7] else cps - 1 - step
            for j in range(N_PAIRS):
                c = (u * len(dirs) + di) * N_PAIRS + j
                s_hi, s_lo = _split2(state[(id(d), j)])
                p_hi, p_lo = p2[c]
                res = jnp.dot(jnp.concatenate([rp[c], p_hi, p_lo], axis=0), rb(s_hi), preferred_element_type=F32)
                d[5][0, u * CHUNK:(u + 1) * CHUNK, lanes[c]] = res[:CHUNK] + y0[c]
                state[(id(d), j)] = (res[CHUNK:2 * CHUNK] + res[2 * CHUNK:]
                                     + jnp.dot(p_hi, rb(s_lo), preferred_element_type=F32) + d02[c])
    for d in dirs:
        for j in range(N_PAIRS):
            d[6][j] = state[(id(d), j)]


def _scan_kernel(raf_ref, kbf_ref, bktf_ref, wcf_ref, vf_ref,
                 rab_ref, kbb_ref, bktb_ref, wcb_ref, vb_ref, s0_ref,
                 yf_ref, yb_ref, sout_ref, sf_ref, sb_ref, *, cps, backward):
    i = pl.program_id(1)

    @pl.when(i == 0)
    def _():
        sf_ref[...] = s0_ref[0]
        sb_ref[...] = jnp.zeros_like(sb_ref)

    dirs = [(raf_ref, kbf_ref, bktf_ref, wcf_ref, vf_ref, yf_ref, sf_ref, True)]
    if backward:
        dirs.append((rab_ref, kbb_ref, bktb_ref, wcb_ref, vb_ref, yb_ref, sb_ref, False))
    else:
        yb_ref[...] = jnp.zeros_like(yb_ref)
    _scan_chains(tuple(dirs), cps)

    @pl.when(i == pl.num_programs(1) - 1)
    def _():
        sout_ref[0] = sf_ref[...]


def _scan(prep_out, s0, cps, backward):
    raf, kbf, bktf, wcf, rab, kbb, bktb, wcb, v = prep_out[:9]
    bsz, length, _ = v.shape
    steps = length // (cps * CHUNK)
    fw = lambda b, i: (b, i, 0)
    bw = lambda b, i: (b, steps - 1 - i, 0)

    def dir_specs(im):
        return [
            pl.BlockSpec((1, 2 * cps * CHUNK, RWKV_WIDTH), im),
            pl.BlockSpec((1, 2 * cps * CHUNK, RWKV_WIDTH), im),
            pl.BlockSpec((1, cps * RWKV_WIDTH, PAIR), im),
            pl.BlockSpec((1, cps * 8, RWKV_WIDTH), im),
            pl.BlockSpec((1, cps * CHUNK, RWKV_WIDTH), im),
        ]

    state_spec = pl.BlockSpec((1, N_PAIRS, RWKV_HEAD, PAIR), lambda b, i: (b, 0, 0, 0))
    return pl.pallas_call(
        functools.partial(_scan_kernel, cps=cps, backward=backward),
        grid=(bsz, steps),
        in_specs=dir_specs(fw) + dir_specs(bw) + [state_spec],
        out_specs=[
            pl.BlockSpec((1, cps * CHUNK, RWKV_WIDTH), fw),
            pl.BlockSpec((1, cps * CHUNK, RWKV_WIDTH), bw),
            state_spec,
        ],
        out_shape=[
            jax.ShapeDtypeStruct((bsz, length, RWKV_WIDTH), F32),
            jax.ShapeDtypeStruct((bsz, length, RWKV_WIDTH), F32),
            jax.ShapeDtypeStruct((bsz, N_PAIRS, RWKV_HEAD, PAIR), F32),
        ],
        scratch_shapes=[pltpu.VMEM((N_PAIRS, RWKV_HEAD, PAIR), F32)] * 2,
        compiler_params=_cparams(("parallel", "arbitrary")),
        name="rwkv_scan",
    )(raf, kbf, bktf, wcf, v, rab, kbb, bktb, wcb, v, s0)


def _attn_kernel(k_ref, qt_ref, vt_ref, km_ref, vtm_ref, lq1_ref, lk1_ref, lq2_ref, lk2_ref, sg_ref,
                 o_ref, m_ref, acc_ref, *, lam_init):
    kv = pl.program_id(2)
    tk = k_ref.shape[1]
    tq = qt_ref.shape[2]
    map_row = lax.broadcasted_iota(jnp.int32, (PAIR, tq), 0) < DIFF_HEAD

    def q_weights(hc):
        qt2 = qt_ref[0, (hc // 2) * PAIR:(hc // 2 + 1) * PAIR, :]
        keep = map_row if hc % 2 == 0 else jnp.logical_not(map_row)
        return jnp.where(keep, qt2, jnp.zeros_like(qt2))

    def key_max(s):
        rows = min(MAX_ROWS, s.shape[0])
        part = jnp.max(s.reshape(s.shape[0] // rows, rows, s.shape[1]), axis=0)
        return jnp.max(part, axis=0, keepdims=True)

    def values_and_ones(vt_h):
        return jnp.concatenate([vt_h, jnp.ones((ONES_ROWS, vt_h.shape[1]), BF16)], axis=0)

    @pl.when(kv == 0)
    def _():
        heads = [slice((hc // 2) * PAIR, (hc // 2 + 1) * PAIR) for hc in range(N_MAPS)]
        s_meta = [jnp.dot(km_ref[0, :, heads[hc]], q_weights(hc), preferred_element_type=F32)
                  for hc in range(N_MAPS)]
        m_meta = [key_max(s) for s in s_meta]
        p_meta = [jnp.exp2(s - m).astype(BF16) for s, m in zip(s_meta, m_meta)]
        for hc in range(N_MAPS):
            acc_ref[hc] = jnp.dot(values_and_ones(vtm_ref[0, heads[hc], :]), p_meta[hc],
                                  preferred_element_type=F32)
            m_ref[hc] = jnp.broadcast_to(m_meta[hc], (8, tq))

    def score(hc):
        head = slice((hc // 2) * PAIR, (hc // 2 + 1) * PAIR)
        return jnp.dot(k_ref[0, :, head], q_weights(hc), preferred_element_type=F32)

    scores = {hc: score(hc) for hc in range(SCORES_AHEAD)}
    for hc in range(N_MAPS):
        if hc + SCORES_AHEAD < N_MAPS:
            scores[hc + SCORES_AHEAD] = score(hc + SCORES_AHEAD)
        s = scores.pop(hc)
        head = slice((hc // 2) * PAIR, (hc // 2 + 1) * PAIR)
        m_prev = m_ref[hc]
        m_next = jnp.maximum(m_prev, key_max(s))
        alpha = jnp.exp2(m_prev - m_next)
        p = jnp.exp2(s - m_next[0:1])
        acc_ref[hc] = alpha[0:1] * acc_ref[hc] + jnp.dot(values_and_ones(vt_ref[0, head, :]), p.astype(BF16),
                                                         preferred_element_type=F32)
        m_ref[hc] = m_next

    @pl.when(kv == pl.num_programs(2) - 1)
    def _():
        lam = (jnp.exp(jnp.sum(lq1_ref[...] * lk1_ref[...], axis=-1, keepdims=True))
               - jnp.exp(jnp.sum(lq2_ref[...] * lk2_ref[...], axis=-1, keepdims=True)) + lam_init)
        gain = jnp.tile(sg_ref[...], (1, tq // PAIR)) * (1.0 - lam_init)
        for hd in range(DIFF_HEADS):
            a1 = acc_ref[2 * hd]
            a2 = acc_ref[2 * hd + 1]
            o = a1[:PAIR] / a1[PAIR:PAIR + 1] - lam * (a2[:PAIR] / a2[PAIR:PAIR + 1])
            ms = jnp.mean(o * o, axis=0, keepdims=True)
            o = o * lax.rsqrt(ms + RMS_EPS) * gain
            o_ref[0, :, hd * PAIR:(hd + 1) * PAIR] = o.T.astype(BF16)


def _attention(k, qt, vt, km, vtm, lam_params, subln_g, lam_init, tq, tk):
    bsz, length, _ = k.shape
    lq1, lk1, lq2, lk2 = lam_params
    return pl.pallas_call(
        functools.partial(_attn_kernel, lam_init=lam_init),
        grid=(bsz, length // tq, length // tk),
        in_specs=[
            pl.BlockSpec((1, tk, DIFF_WIDTH), lambda b, i, j: (b, j, 0)),
            pl.BlockSpec((1, DIFF_WIDTH, tq), lambda b, i, j: (b, 0, i)),
            pl.BlockSpec((1, DIFF_WIDTH, tk), lambda b, i, j: (b, 0, j)),
            _const_spec((1, N_META, DIFF_WIDTH)),
            _const_spec((1, DIFF_WIDTH, N_META)),
            _const_spec((1, DIFF_HEAD)),
            _const_spec((1, DIFF_HEAD)),
            _const_spec((1, DIFF_HEAD)),
            _const_spec((1, DIFF_HEAD)),
            _const_spec((PAIR, PAIR)),
        ],
        out_specs=pl.BlockSpec((1, tq, DIFF_WIDTH), lambda b, i, j: (b, i, 0)),
        out_shape=jax.ShapeDtypeStruct((bsz, length, DIFF_WIDTH), BF16),
        scratch_shapes=[pltpu.VMEM((N_MAPS, 8, tq), F32), pltpu.VMEM((N_MAPS, PAIR + ONES_ROWS, tq), F32)],
        compiler_params=_cparams(("parallel", "parallel", "arbitrary")),
        name="diff_attn",
    )(k, qt, vt, km, vtm, lq1, lk1, lq2, lk2, subln_g)


def _out_kernel(x_ref, yf_ref, yb_ref, bonus_ref, g_ref, od_ref, lnw_ref, lnb_ref, gs_ref,
                wo_ref, g2n_ref, wg_ref, wu_ref, wd_ref, o_ref):
    gs = gs_ref[...]
    y = yf_ref[0] + yb_ref[0]
    mu = _split_dot(y, gs, 2) * (1.0 / RWKV_HEAD)
    yc = y - mu
    var = _split_dot(yc * yc, gs, 2) * (1.0 / RWKV_HEAD)
    yn = yc * lax.rsqrt(var + GN_EPS) * lnw_ref[...] + lnb_ref[...]
    o_rwkv = (yn + bonus_ref[0].astype(F32)) * g_ref[0].astype(F32)
    x1 = (x_ref[0]
          + jnp.dot(o_rwkv.astype(BF16), wo_ref[:RWKV_WIDTH, :], preferred_element_type=F32)
          + jnp.dot(od_ref[0], wo_ref[RWKV_WIDTH:, :], preferred_element_type=F32))
    ms = jnp.mean(x1 * x1, axis=-1, keepdims=True)
    h2 = (x1 * lax.rsqrt(ms + RMS_EPS) * g2n_ref[...]).astype(BF16)
    gate = jnp.dot(h2, wg_ref[...], preferred_element_type=F32)
    up = jnp.dot(h2, wu_ref[...], preferred_element_type=F32)
    act = (gate * jax.nn.sigmoid(gate) * up).astype(BF16)
    o_ref[0] = x1 + jnp.dot(act, wd_ref[...], preferred_element_type=F32)


def _out(x, yf, yb, bonus, g, od, consts, tm):
    bsz, length, _ = x.shape
    lnw, lnb, gsum, wo, g2n, wg, wu, wd = consts
    tok = lambda b, i: (b, i, 0)
    half = pl.BlockSpec((1, tm, RWKV_WIDTH), tok)
    return pl.pallas_call(
        _out_kernel,
        grid=(bsz, length // tm),
        in_specs=[
            pl.BlockSpec((1, tm, D_MODEL), tok), half, half, half, half, half,
            _const_spec((1, RWKV_WIDTH)),
            _const_spec((1, RWKV_WIDTH)),
            _const_spec((PAIR, PAIR)),
            _const_spec((D_MODEL, D_MODEL)),
            _const_spec((1, D_MODEL)),
            _const_spec((D_MODEL, D_FF)),
            _const_spec((D_MODEL, D_FF)),
            _const_spec((D_FF, D_MODEL)),
        ],
        out_specs=pl.BlockSpec((1, tm, D_MODEL), tok),
        out_shape=jax.ShapeDtypeStruct((bsz, length, D_MODEL), F32),
        compiler_params=_cparams(("parallel", "parallel")),
        name="out_ffn",
    )(x, yf, yb, bonus, g, od, lnw, lnb, gsum, wo, g2n, wg, wu, wd)


def _rope_tables(length, offset, tm):
    inv = ROPE_THETA ** (-jnp.arange(0, DIFF_HEAD, 2, dtype=F32) / DIFF_HEAD)

    def tables(pos):
        ang = pos[:, None] * inv[None, :]
        sin = jnp.sin(ang)
        return jnp.concatenate([jnp.cos(ang)] * 4, axis=-1), jnp.concatenate([-sin, sin, -sin, sin], axis=-1)

    cos_a, sin_a = tables(jnp.arange(0, length, tm, dtype=F32) + offset)
    cos_b, sin_b = tables(jnp.arange(tm, dtype=F32))
    return cos_a[:, None, :], sin_a[:, None, :], cos_b, sin_b


def _block_diag2(a, b):
    za = jnp.zeros_like(a)
    return jnp.concatenate([jnp.concatenate([a, za], axis=1), jnp.concatenate([za, b], axis=1)], axis=0)


def _tile_for(length, pref):
    t = pref
    while length % t:
        t //= 2
    return t


def kernel(x_prompt, x_sample, meta_tokens, norm1_g, w_in, shift_mu_prev, shift_mu_next, w0_f, w2_f, w0_b, w2_b, a0_f, a2_f, a0_b, a2_b, g2, k_k, k_a, r_k, ln_x_w, ln_x_b, q_norm_g, k_norm_g, lam_q1, lam_k1, lam_q2, lam_k2, subln_g, w_out, norm2_g, w_gate, w_up, w_down):
    lam_init = 0.8 - 0.6 * math.exp(-0.3 * 0)
    row = lambda t: t.reshape(1, -1).astype(F32)
    gi = jnp.arange(PAIR) // RWKV_HEAD
    gsum = (gi[:, None] == gi[None, :]).astype(BF16)

    g1 = row(norm1_g[0])
    w_in_b = w_in[0].astype(BF16)
    qg = row(jnp.tile(q_norm_g[0], N_MAPS))
    kg = row(jnp.tile(k_norm_g[0], N_MAPS))
    prep_consts = (
        row(shift_mu_prev[0]), row(shift_mu_next[0]),
        row(jnp.concatenate([w0_f[0], w0_b[0]])), jnp.stack(_split2(_block_diag2(w2_f[0], w2_b[0]).astype(F32))),
        row(jnp.concatenate([a0_f[0], a0_b[0]])), jnp.stack(_split2(_block_diag2(a2_f[0], a2_b[0]).astype(F32))),
        g2[0].astype(BF16), row(k_k[0]), row(k_a[0]), row(r_k[0]), gsum,
    )
    lam_params = (row(lam_q1[0]), row(lam_k1[0]), row(lam_q2[0]), row(lam_k2[0]))
    sg = jnp.broadcast_to(subln_g[0].astype(F32)[:, None], (PAIR, PAIR))
    out_consts = (row(ln_x_w[0]), row(ln_x_b[0]), gsum, w_out[0].astype(BF16), row(norm2_g[0]),
                  w_gate[0].astype(BF16), w_up[0].astype(BF16), w_down[0].astype(BF16))

    meta_x = jnp.zeros((1, META_PAD, D_MODEL), F32).at[0, :N_META].set(meta_tokens.astype(F32))
    p_meta, _, k_meta, vt_meta = _inproj(meta_x, _rope_tables(META_PAD, 0, META_PAD), g1, w_in_b, qg, kg, gsum,
                                         META_PAD)
    prev8 = p_meta[0, N_META - 8:N_META]
    zero8 = jnp.zeros_like(prev8)

    def group(x):
        bsz, length, _ = x.shape
        tm = _tile_for(length, 512)
        p, qt, k, vt = _inproj(x, _rope_tables(length, N_META, tm), g1, w_in_b, qg, kg, gsum, tm)
        pm = jnp.concatenate(
            [jnp.broadcast_to(p_meta[:, :N_META], (bsz, N_META, SHIFT_COLS)),
             p[:, :CHUNK - N_META]], axis=1)
        meta_prep = _prep(pm, zero8, prep_consts, CHUNK, N_META)
        zero_state = jnp.zeros((bsz, N_PAIRS, RWKV_HEAD, PAIR), F32)
        s_meta = _scan(meta_prep, zero_state, 1, False)[2]
        prep_out = _prep(p, prev8, prep_consts, _tile_for(length, 256), None)
        yf, yb, _ = _scan(prep_out, s_meta, 8, True)
        od = _attention(k, qt, vt, k_meta[:, :N_META], vt_meta[:, :, :N_META], lam_params, sg, lam_init,
                        _tile_for(length, 1024), _tile_for(length, 1024))
        return _out(x, yf, yb, prep_out[10], prep_out[9], od, out_consts, _tile_for(length, 512))

    return (group(x_prompt), group(x_sample))
```

```python
import functools
import math

import jax
import jax.numpy as jnp
from jax import lax
from jax.experimental import pallas as pl
from jax.experimental.pallas import tpu as pltpu

F32 = jnp.float32
BF16 = jnp.bfloat16

D_MODEL = 1024
N_META = 16
RWKV_WIDTH = 512
RWKV_HEAD = 64
DIFF_WIDTH = 512
DIFF_HEAD = 64
DIFF_HEADS = 4
N_MAPS = 2 * DIFF_HEADS
GATE_LORA = 128
LORA_PAIR = 128
SHIFT_COLS = 3 * RWKV_WIDTH + GATE_LORA + 2 * LORA_PAIR
IN_COLS = SHIFT_COLS + 3 * DIFF_WIDTH
D_FF = 2816
ROPE_THETA = 10000.0
RMS_EPS = 1e-6
GN_EPS = 64e-5
CHUNK = 64
SUB = 16
PAIR = 128
N_PAIRS = RWKV_WIDTH // PAIR
META_PAD = 128
SCORES_AHEAD = 2
ONES_ROWS = 16
MAX_ROWS = 64
VMEM_LIMIT_BYTES = 56 * 1024 * 1024


def _cparams(sem):
    return pltpu.CompilerParams(dimension_semantics=sem, vmem_limit_bytes=VMEM_LIMIT_BYTES)


def _const_spec(shape):
    nd = len(shape)
    return pl.BlockSpec(shape, lambda *_: (0,) * nd, pipeline_mode=pl.Buffered(1))


def _mm(a, b):
    return jnp.dot(a.astype(BF16), b.astype(BF16), preferred_element_type=F32)


def _mm_nt(a, b):
    return lax.dot_general(a.astype(BF16), b.astype(BF16), (((1,), (1,)), ((), ())),
                           preferred_element_type=F32)


def _split_dot(x, m, terms):
    outs = []
    for c in range(x.shape[1] // PAIR):
        acc = None
        rem = x[:, c * PAIR:(c + 1) * PAIR]
        for t in range(terms):
            piece = rem.astype(BF16)
            part = jnp.dot(piece, m, preferred_element_type=F32)
            acc = part if acc is None else acc + part
            if t + 1 < terms:
                rem = rem - piece.astype(F32)
        outs.append(acc)
    return jnp.concatenate(outs, axis=1)


def _split2(x):
    hi = x.astype(BF16)
    return hi, (x - hi.astype(F32)).astype(BF16)


def _mm3(a_hi, a_lo, b_hi, b_lo):
    dot = functools.partial(jnp.dot, preferred_element_type=F32)
    return dot(a_hi, b_hi) + (dot(a_lo, b_hi) + dot(a_hi, b_lo))


def _tri_dot(tri, x):
    hi, lo = _split2(x)
    return jnp.dot(tri, hi, preferred_element_type=F32) + jnp.dot(tri, lo, preferred_element_type=F32)


def _inproj_kernel(x_ref, g1_ref, w_ref, cos_a_ref, sin_a_ref, cos_b_ref, sin_b_ref, qg_ref, kg_ref, gs_ref,
                   p_ref, qt_ref, k_ref, vt_ref):
    x = x_ref[0]
    ms = jnp.mean(x * x, axis=-1, keepdims=True)
    h = (x * lax.rsqrt(ms + RMS_EPS) * g1_ref[...]).astype(BF16)
    p_ref[0] = jnp.dot(h, w_ref[:, :SHIFT_COLS], preferred_element_type=F32)
    qkv = jnp.dot(h, w_ref[:, SHIFT_COLS:], preferred_element_type=F32)
    q = qkv[:, :DIFF_WIDTH]
    k = qkv[:, DIFF_WIDTH:2 * DIFF_WIDTH]
    v = qkv[:, 2 * DIFF_WIDTH:]

    cos_a, sin_a, cos_b, sin_b = cos_a_ref[0], sin_a_ref[0], cos_b_ref[...], sin_b_ref[...]
    reps = DIFF_WIDTH // cos_b.shape[1]
    cos = jnp.tile(cos_a * cos_b - sin_a * sin_b, (1, reps))
    sin = jnp.tile(sin_a * cos_b + cos_a * sin_b, (1, reps))
    lane = lax.broadcasted_iota(jnp.int32, q.shape, 1)
    first_half = (lane % DIFF_HEAD) < (DIFF_HEAD // 2)
    gs = gs_ref[...]

    def norm_rope(t, g):
        ss = _split_dot(t * t, gs, 1)
        t = t * lax.rsqrt(ss * (1.0 / DIFF_HEAD) + RMS_EPS) * g
        rot = jnp.where(first_half,
                        pltpu.roll(t, DIFF_WIDTH - DIFF_HEAD // 2, 1),
                        pltpu.roll(t, DIFF_HEAD // 2, 1))
        return t * cos + rot * sin

    qr = norm_rope(q, qg_ref[...]) * (math.log2(math.e) / math.sqrt(DIFF_HEAD))
    kr = norm_rope(k, kg_ref[...])
    qt_ref[0] = qr.T.astype(BF16)
    k_ref[0] = kr.astype(BF16)
    vt_ref[0] = v.T.astype(BF16)


def _inproj(x, rope, g1, w_in, qg, kg, gsum, tm):
    bsz, length, _ = x.shape
    grid = (bsz, length // tm)
    cos_a, sin_a, cos_b, sin_b = rope
    return pl.pallas_call(
        _inproj_kernel,
        grid=grid,
        in_specs=[
            pl.BlockSpec((1, tm, D_MODEL), lambda b, i: (b, i, 0)),
            _const_spec((1, D_MODEL)),
            _const_spec((D_MODEL, IN_COLS)),
            pl.BlockSpec((1, 1, PAIR), lambda b, i: (i, 0, 0)),
            pl.BlockSpec((1, 1, PAIR), lambda b, i: (i, 0, 0)),
            _const_spec((tm, PAIR)),
            _const_spec((tm, PAIR)),
            _const_spec((1, DIFF_WIDTH)),
            _const_spec((1, DIFF_WIDTH)),
            _const_spec((PAIR, PAIR)),
        ],
        out_specs=[
            pl.BlockSpec((1, tm, SHIFT_COLS), lambda b, i: (b, i, 0)),
            pl.BlockSpec((1, DIFF_WIDTH, tm), lambda b, i: (b, 0, i)),
            pl.BlockSpec((1, tm, DIFF_WIDTH), lambda b, i: (b, i, 0)),
            pl.BlockSpec((1, DIFF_WIDTH, tm), lambda b, i: (b, 0, i)),
        ],
        out_shape=[
            jax.ShapeDtypeStruct((bsz, length, SHIFT_COLS), F32),
            jax.ShapeDtypeStruct((bsz, DIFF_WIDTH, length), BF16),
            jax.ShapeDtypeStruct((bsz, length, DIFF_WIDTH), BF16),
            jax.ShapeDtypeStruct((bsz, DIFF_WIDTH, length), BF16),
        ],
        compiler_params=_cparams(("parallel", "parallel")),
        name="inproj",
    )(x, g1, w_in, cos_a, sin_a, cos_b, sin_b, qg, kg, gsum)


def _prep_kernel(p_ref, ph_ref, nh_ref, p0_ref, mup_ref, mun_ref, w0_ref, w2_ref, a0_ref, a2_ref,
                 g2_ref, kk_ref, ka_ref, rk_ref, gs_ref,
                 raf_ref, kbf_ref, bktf_ref, wcf_ref, rab_ref, kbb_ref, bktb_ref, wcb_ref,
                 v_ref, g_ref, bonus_ref, *, valid_rows):
    i = pl.program_id(1)
    last = pl.num_programs(1) - 1
    p = p_ref[0]
    tm = p.shape[0]
    row = lax.broadcasted_iota(jnp.int32, p.shape, 0)
    prev_row = jnp.where(i == 0, p0_ref[7:8, :], ph_ref[0, 7:8, :])
    next_row = jnp.where(i == last, jnp.zeros_like(prev_row), nh_ref[0, 0:1, :])
    p_prev = jnp.where(row == 0, prev_row, pltpu.roll(p, 1, 0))
    p_next = jnp.where(row == tm - 1, next_row, pltpu.roll(p, tm - 1, 0))
    mu_prev, mu_next = mup_ref[...], mun_ref[...]
    z = p * (1.0 - mu_prev - mu_next) + mu_prev * p_prev + mu_next * p_next

    w = RWKV_WIDTH
    r = z[:, :w]
    k = z[:, w:2 * w]
    v = z[:, 2 * w:3 * w]
    gd = z[:, 3 * w:3 * w + GATE_LORA]
    wd = z[:, 3 * w + GATE_LORA:3 * w + GATE_LORA + LORA_PAIR]
    ad = z[:, 3 * w + GATE_LORA + LORA_PAIR:]

    wl = w0_ref[...] + _mm3(*_split2(jnp.tanh(wd)), w2_ref[0], w2_ref[1])
    lw = (-math.exp(-0.5) * math.log2(math.e)) * jax.nn.sigmoid(wl)
    al = a0_ref[...] + _mm3(*_split2(ad), a2_ref[0], a2_ref[1])
    iclr = jax.nn.sigmoid(al)
    g = _mm(jax.nn.sigmoid(gd), g2_ref[...])

    gs = gs_ref[...]
    kk = k * kk_ref[...]
    kk = kk * lax.rsqrt(jnp.maximum(_split_dot(kk * kk, gs, 1), 1e-24))
    k_a = ka_ref[...]
    a_f = iclr[:, :w]
    a_b = iclr[:, w:]
    k_f = k * (1.0 + (a_f - 1.0) * k_a)
    k_b = k * (1.0 + (a_b - 1.0) * k_a)
    bonus = _split_dot(r * k_f * rk_ref[...], gs, 1) * v
    a_neg = -kk
    b_f = kk * a_f
    b_b = kk * a_b
    lw_f = lw[:, :w]
    lw_b = lw[:, w:]

    if valid_rows is not None:
        keep = lax.broadcasted_iota(jnp.int32, r.shape, 0) < valid_rows
        zero = jnp.zeros_like(r)
        r, v, a_neg = (jnp.where(keep, t, zero) for t in (r, v, a_neg))
        k_f, k_b, b_f, b_b = (jnp.where(keep, t, zero) for t in (k_f, k_b, b_f, b_b))
        lw_f, lw_b = (jnp.where(keep, t, zero) for t in (lw_f, lw_b))

    v_ref[0] = v.astype(BF16)
    g_ref[0] = g.astype(BF16)
    bonus_ref[0] = bonus.astype(BF16)

    ti = lax.broadcasted_iota(jnp.int32, (CHUNK, CHUNK), 0)
    si = lax.broadcasted_iota(jnp.int32, (CHUNK, CHUNK), 1)
    tri_f = jnp.where(si <= ti, 1.0, 0.0).astype(BF16)
    tri_b = jnp.where(si >= ti, 1.0, 0.0).astype(BF16)

    for c in range(tm // CHUNK):
        rows = slice(c * CHUNK, (c + 1) * CHUNK)
        r_c, v_c, a_c = r[rows], v[rows], a_neg[rows]
        for (lw_d, k_d, b_d, tri, fwd, ra_ref, kb_ref, bkt_ref, wc_ref) in (
                (lw_f, k_f, b_f, tri_f, True, raf_ref, kbf_ref, bktf_ref, wcf_ref),
                (lw_b, k_b, b_b, tri_b, False, rab_ref, kbb_ref, bktb_ref, wcb_ref)):
            lw_c = lw_d[rows]
            cum = _tri_dot(tri, lw_c)
            tot = cum[CHUNK - 1:CHUNK] if fwd else cum[0:1]
            e_excl = jnp.exp2(cum - lw_c)
            e_read = jnp.exp2(cum) if fwd else e_excl
            e_inv = jnp.exp2(-cum)
            e_end = jnp.exp2(tot - cum)
            ra_ref[0, 2 * c * CHUNK:(2 * c + 1) * CHUNK] = (r_c * e_read).astype(BF16)
            ra_ref[0, (2 * c + 1) * CHUNK:(2 * c + 2) * CHUNK] = (a_c * e_excl).astype(BF16)
            kb_ref[0, 2 * c * CHUNK:(2 * c + 1) * CHUNK] = (k_d[rows] * e_inv).astype(BF16)
            kb_ref[0, (2 * c + 1) * CHUNK:(2 * c + 2) * CHUNK] = (b_d[rows] * e_inv).astype(BF16)
            bk = jnp.concatenate([b_d[rows] * e_end, k_d[rows] * e_end], axis=0)
            bkt_ref[0, c * w:(c + 1) * w] = bk.T.astype(BF16)
            wc_ref[0, 8 * c:8 * (c + 1)] = jnp.broadcast_to(jnp.exp2(tot), (8, w))


def _prep(p, prev8, consts, tm, valid_rows):
    bsz, length, _ = p.shape
    nt = length // tm
    nc = length // CHUNK
    cpt = tm // CHUNK
    hb = tm // 8
    n8 = length // 8
    mup, mun, w0, w2, a0, a2, g2, k_k, k_a, r_k, gsum = consts
    tok = lambda b, i: (b, i, 0)
    dir_specs = [
        pl.BlockSpec((1, 2 * tm, RWKV_WIDTH), tok),
        pl.BlockSpec((1, 2 * tm, RWKV_WIDTH), tok),
        pl.BlockSpec((1, cpt * RWKV_WIDTH, PAIR), tok),
        pl.BlockSpec((1, cpt * 8, RWKV_WIDTH), tok),
    ]
    dir_shapes = [
        jax.ShapeDtypeStruct((bsz, 2 * length, RWKV_WIDTH), BF16),
        jax.ShapeDtypeStruct((bsz, 2 * length, RWKV_WIDTH), BF16),
        jax.ShapeDtypeStruct((bsz, nc * RWKV_WIDTH, PAIR), BF16),
        jax.ShapeDtypeStruct((bsz, nc * 8, RWKV_WIDTH), F32),
    ]
    tok_spec = pl.BlockSpec((1, tm, RWKV_WIDTH), tok)
    tok_shape = jax.ShapeDtypeStruct((bsz, length, RWKV_WIDTH), BF16)
    return pl.pallas_call(
        functools.partial(_prep_kernel, valid_rows=valid_rows),
        grid=(bsz, nt),
        in_specs=[
            pl.BlockSpec((1, tm, SHIFT_COLS), tok),
            pl.BlockSpec((1, 8, SHIFT_COLS), lambda b, i: (b, jnp.maximum(i * hb - 1, 0), 0)),
            pl.BlockSpec((1, 8, SHIFT_COLS), lambda b, i: (b, jnp.minimum((i + 1) * hb, n8 - 1), 0)),
            _const_spec((8, SHIFT_COLS)),
            _const_spec((1, SHIFT_COLS)),
            _const_spec((1, SHIFT_COLS)),
            _const_spec((1, 2 * RWKV_WIDTH)),
            _const_spec((2, LORA_PAIR, 2 * RWKV_WIDTH)),
            _const_spec((1, 2 * RWKV_WIDTH)),
            _const_spec((2, LORA_PAIR, 2 * RWKV_WIDTH)),
            _const_spec((GATE_LORA, RWKV_WIDTH)),
            _const_spec((1, RWKV_WIDTH)),
            _const_spec((1, RWKV_WIDTH)),
            _const_spec((1, RWKV_WIDTH)),
            _const_spec((PAIR, PAIR)),
        ],
        out_specs=dir_specs + dir_specs + [tok_spec, tok_spec, tok_spec],
        out_shape=dir_shapes + dir_shapes + [tok_shape, tok_shape, tok_shape],
        compiler_params=_cparams(("parallel", "parallel")),
        name="rwkv_prep",
    )(p, p, p, prev8, mup, mun, w0, w2, a0, a2, g2, k_k, k_a, r_k, gsum)


def _row_blocks(x, lane_lo):
    zero = jnp.zeros_like(x)
    return jnp.concatenate([jnp.where(lane_lo, x, zero), jnp.where(lane_lo, zero, x)], axis=0)


def _scan_chains(dirs, cps):
    ti = lax.broadcasted_iota(jnp.int32, (CHUNK, PAIR), 0)
    li = lax.broadcasted_iota(jnp.int32, (CHUNK, PAIR), 1)
    si = li % CHUNK
    lane_lo = li < CHUNK
    lower, lower_eq, upper = ti > si, ti >= si, ti < si
    same_sub = (ti // SUB) == (si // SUB)
    eye_w = jnp.where(ti == si, 1.0, 0.0).astype(F32)
    rb = lambda t: _row_blocks(t, lane_lo)
    each = lambda fn, *cols: [fn(*vals) for vals in zip(*cols)]

    chains = [(d, j, u) for u in range(cps) for d in dirs for j in range(N_PAIRS)]
    lanes = [slice(j * PAIR, (j + 1) * PAIR) for _, j, _ in chains]
    mask_a = [lower if d[7] else upper for d, _, _ in chains]
    mask_r = [lower_eq if d[7] else upper for d, _, _ in chains]
    ra = [d[0][0, 2 * u * CHUNK:2 * (u + 1) * CHUNK, ln] for (d, _, u), ln in zip(chains, lanes)]
    kb = [d[1][0, 2 * u * CHUNK:2 * (u + 1) * CHUNK, ln] for (d, _, u), ln in zip(chains, lanes)]
    v2 = [d[4][0, u * CHUNK:(u + 1) * CHUNK, ln] for (d, _, u), ln in zip(chains, lanes)]
    bkt = [d[2][0, u * RWKV_WIDTH + j * PAIR:u * RWKV_WIDTH + (j + 1) * PAIR, :] for d, j, u in chains]
    wc = [d[3][0, 8 * u:8 * u + 1, ln] for (d, _, u), ln in zip(chains, lanes)]
    rbv = each(rb, v2)

    rows2 = lambda a, b: jnp.concatenate([a, b], axis=0)
    cols2 = lambda a, b: jnp.concatenate([a, b], axis=1)
    top, bot = (lambda t: t[:CHUNK]), (lambda t: t[CHUNK:])
    left, right = (lambda t: t[:, :PAIR]), (lambda t: t[:, PAIR:])

    akb = each(lambda a, b: _mm_nt(a, rows2(rb(b[:CHUNK]), rb(b[CHUNK:]))), ra, kb)
    a_rk = each(lambda m, t: jnp.where(m, left(top(t)), 0.0), mask_r, akb)
    a_ak = each(lambda m, t: jnp.where(m, left(bot(t)), 0.0), mask_a, akb)
    a_rb = each(lambda m, t: jnp.where(m, right(top(t)), 0.0), mask_r, akb)
    n_all = each(lambda m, t: jnp.where(m, right(bot(t)), 0.0), mask_a, akb)

    d1 = each(lambda t: jnp.where(same_sub, t, 0.0), n_all)
    e1 = each(lambda t, d: t - d, n_all, d1)
    d2 = each(lambda d: _mm(d, rb(d)), d1)
    arav = each(lambda a, b, r: _mm(rows2(a, b), r), a_rk, a_ak, rbv)
    t16 = each(lambda d: eye_w + d, d1)
    x2 = each(lambda t, d: _mm(rows2(t, d), rb(d)), t16, d2)
    t16 = each(lambda t, x: t + top(x), t16, x2)
    x4 = each(lambda t, x: _mm(rows2(t, bot(x)), rb(bot(x))), t16, x2)
    t16 = each(lambda t, x: t + top(x), t16, x4)
    t16 = each(lambda t, x: t + _mm(t, rb(bot(x))), t16, x4)
    m1 = each(lambda t, e: _mm(t, rb(e)), t16, e1)
    x6 = each(lambda m, t: _mm(m, cols2(rb(m), rb(t))), m1, t16)
    yy = each(lambda t, x: t + right(x), t16, x6)
    tinv = each(lambda y, x: y + _mm(left(x), rb(y)), yy, x6)

    tu = each(lambda t, r, a: _mm(t, cols2(rb(bot(r)), rb(bot(a).astype(BF16)))), tinv, ra, arav)
    ar = each(lambda a, t: _mm(a, cols2(rb(left(t)), rb(right(t)))), a_rb, tu)
    rp = each(lambda r, a: (top(r).astype(F32) + left(a)).astype(BF16), ra, ar)
    y0 = each(lambda a, w: right(a) + top(w), ar, arav)
    stack = each(lambda t, v: jnp.concatenate(
        [t, jnp.concatenate([jnp.zeros((CHUNK, PAIR), F32), v.astype(F32)], axis=1)], axis=0), tu, v2)
    pd = each(_mm, bkt, stack)
    heads_w = lambda t: jnp.where(lane_lo, top(t), bot(t))
    p2 = each(lambda t, w: _split2(heads_w(left(t)) + eye_w * w), pd, wc)
    d02 = each(lambda t: heads_w(right(t)), pd)

    state = {(id(d), j): d[6][j] for d in dirs for j in range(N_PAIRS)}
    for step in range(cps):
        for di, d in enumerate(dirs):
            u = step if d[7] else cps - 1 - step
            for j in range(N_PAIRS):
                c = (u * len(dirs) + di) * N_PAIRS + j
                s_hi, s_lo = _split2(state[(id(d), j)])
                p_hi, p_lo = p2[c]
                res = jnp.dot(jnp.concatenate([rp[c], p_hi, p_lo], axis=0), rb(s_hi), preferred_element_type=F32)
                d[5][0, u * CHUNK:(u + 1) * CHUNK, lanes[c]] = res[:CHUNK] + y0[c]
                state[(id(d), j)] = (res[CHUNK:2 * CHUNK] + res[2 * CHUNK:]
                                     + jnp.dot(p_hi, rb(s_lo), preferred_element_type=F32) + d02[c])
    for d in dirs:
        for j in range(N_PAIRS):
            d[6][j] = state[(id(d), j)]


def _scan_kernel(raf_ref, kbf_ref, bktf_ref, wcf_ref, vf_ref,
                 rab_ref, kbb_ref, bktb_ref, wcb_ref, vb_ref, s0_ref,
                 yf_ref, yb_ref, sout_ref, sf_ref, sb_ref, *, cps, backward):
    i = pl.program_id(1)

    @pl.when(i == 0)
    def _():
        sf_ref[...] = s0_ref[0]
        sb_ref[...] = jnp.zeros_like(sb_ref)

    dirs = [(raf_ref, kbf_ref, bktf_ref, wcf_ref, vf_ref, yf_ref, sf_ref, True)]
    if backward:
        dirs.append((rab_ref, kbb_ref, bktb_ref, wcb_ref, vb_ref, yb_ref, sb_ref, False))
    else:
        yb_ref[...] = jnp.zeros_like(yb_ref)
    _scan_chains(tuple(dirs), cps)

    @pl.when(i == pl.num_programs(1) - 1)
    def _():
        sout_ref[0] = sf_ref[...]


def _scan(prep_out, s0, cps, backward):
    raf, kbf, bktf, wcf, rab, kbb, bktb, wcb, v = prep_out[:9]
    bsz, length, _ = v.shape
    steps = length // (cps * CHUNK)
    fw = lambda b, i: (b, i, 0)
    bw = lambda b, i: (b, steps - 1 - i, 0)

    def dir_specs(im):
        return [
            pl.BlockSpec((1, 2 * cps * CHUNK, RWKV_WIDTH), im),
            pl.BlockSpec((1, 2 * cps * CHUNK, RWKV_WIDTH), im),
            pl.BlockSpec((1, cps * RWKV_WIDTH, PAIR), im),
            pl.BlockSpec((1, cps * 8, RWKV_WIDTH), im),
            pl.BlockSpec((1, cps * CHUNK, RWKV_WIDTH), im),
        ]

    state_spec = pl.BlockSpec((1, N_PAIRS, RWKV_HEAD, PAIR), lambda b, i: (b, 0, 0, 0))
    return pl.pallas_call(
        functools.partial(_scan_kernel, cps=cps, backward=backward),
        grid=(bsz, steps),
        in_specs=dir_specs(fw) + dir_specs(bw) + [state_spec],
        out_specs=[
            pl.BlockSpec((1, cps * CHUNK, RWKV_WIDTH), fw),
            pl.BlockSpec((1, cps * CHUNK, RWKV_WIDTH), bw),
            state_spec,
        ],
        out_shape=[
            jax.ShapeDtypeStruct((bsz, length, RWKV_WIDTH), F32),
            jax.ShapeDtypeStruct((bsz, length, RWKV_WIDTH), F32),
            jax.ShapeDtypeStruct((bsz, N_PAIRS, RWKV_HEAD, PAIR), F32),
        ],
        scratch_shapes=[pltpu.VMEM((N_PAIRS, RWKV_HEAD, PAIR), F32)] * 2,
        compiler_params=_cparams(("parallel", "arbitrary")),
        name="rwkv_scan",
    )(raf, kbf, bktf, wcf, v, rab, kbb, bktb, wcb, v, s0)


def _attn_kernel(k_ref, qt_ref, vt_ref, km_ref, vtm_ref, lq1_ref, lk1_ref, lq2_ref, lk2_ref, sg_ref,
                 o_ref, m_ref, acc_ref, *, lam_init):
    kv = pl.program_id(2)
    tk = k_ref.shape[1]
    tq = qt_ref.shape[2]
    map_row = lax.broadcasted_iota(jnp.int32, (PAIR, tq), 0) < DIFF_HEAD

    def q_weights(hc):
        qt2 = qt_ref[0, (hc // 2) * PAIR:(hc // 2 + 1) * PAIR, :]
        keep = map_row if hc % 2 == 0 else jnp.logical_not(map_row)
        return jnp.where(keep, qt2, jnp.zeros_like(qt2))

    def key_max(s):
        rows = min(MAX_ROWS, s.shape[0])
        part = jnp.max(s.reshape(s.shape[0] // rows, rows, s.shape[1]), axis=0)
        return jnp.max(part, axis=0, keepdims=True)

    def values_and_ones(vt_h):
        return jnp.concatenate([vt_h, jnp.ones((ONES_ROWS, vt_h.shape[1]), BF16)], axis=0)

    @pl.when(kv == 0)
    def _():
        heads = [slice((hc // 2) * PAIR, (hc // 2 + 1) * PAIR) for hc in range(N_MAPS)]
        s_meta = [jnp.dot(km_ref[0, :, heads[hc]], q_weights(hc), preferred_element_type=F32)
                  for hc in range(N_MAPS)]
        m_meta = [key_max(s) for s in s_meta]
        p_meta = [jnp.exp2(s - m).astype(BF16) for s, m in zip(s_meta, m_meta)]
        for hc in range(N_MAPS):
            acc_ref[hc] = jnp.dot(values_and_ones(vtm_ref[0, heads[hc], :]), p_meta[hc],
                                  preferred_element_type=F32)
            m_ref[hc] = jnp.broadcast_to(m_meta[hc], (8, tq))

    def score(hc):
        head = slice((hc // 2) * PAIR, (hc // 2 + 1) * PAIR)
        return jnp.dot(k_ref[0, :, head], q_weights(hc), preferred_element_type=F32)

    scores = {hc: score(hc) for hc in range(SCORES_AHEAD)}
    for hc in range(N_MAPS):
        if hc + SCORES_AHEAD < N_MAPS:
            scores[hc + SCORES_AHEAD] = score(hc + SCORES_AHEAD)
        s = scores.pop(hc)
        head = slice((hc // 2) * PAIR, (hc // 2 + 1) * PAIR)
        m_prev = m_ref[hc]
        m_next = jnp.maximum(m_prev, key_max(s))
        alpha = jnp.exp2(m_prev - m_next)
        p = jnp.exp2(s - m_next[0:1])
        acc_ref[hc] = alpha[0:1] * acc_ref[hc] + jnp.dot(values_and_ones(vt_ref[0, head, :]), p.astype(BF16),
                                                         preferred_element_type=F32)
        m_ref[hc] = m_next

    @pl.when(kv == pl.num_programs(2) - 1)
    def _():
        lam = (jnp.exp(jnp.sum(lq1_ref[...] * lk1_ref[...], axis=-1, keepdims=True))
               - jnp.exp(jnp.sum(lq2_ref[...] * lk2_ref[...], axis=-1, keepdims=True)) + lam_init)
        gain = jnp.tile(sg_ref[...], (1, tq // PAIR)) * (1.0 - lam_init)
        for hd in range(DIFF_HEADS):
            a1 = acc_ref[2 * hd]
            a2 = acc_ref[2 * hd + 1]
            o = a1[:PAIR] / a1[PAIR:PAIR + 1] - lam * (a2[:PAIR] / a2[PAIR:PAIR + 1])
            ms = jnp.mean(o * o, axis=0, keepdims=True)
            o = o * lax.rsqrt(ms + RMS_EPS) * gain
            o_ref[0, :, hd * PAIR:(hd + 1) * PAIR] = o.T.astype(BF16)


def _attention(k, qt, vt, km, vtm, lam_params, subln_g, lam_init, tq, tk):
    bsz, length, _ = k.shape
    lq1, lk1, lq2, lk2 = lam_params
    return pl.pallas_call(
        functools.partial(_attn_kernel, lam_init=lam_init),
        grid=(bsz, length // tq, length // tk),
        in_specs=[
            pl.BlockSpec((1, tk, DIFF_WIDTH), lambda b, i, j: (b, j, 0)),
            pl.BlockSpec((1, DIFF_WIDTH, tq), lambda b, i, j: (b, 0, i)),
            pl.BlockSpec((1, DIFF_WIDTH, tk), lambda b, i, j: (b, 0, j)),
            _const_spec((1, N_META, DIFF_WIDTH)),
            _const_spec((1, DIFF_WIDTH, N_META)),
            _const_spec((1, DIFF_HEAD)),
            _const_spec((1, DIFF_HEAD)),
            _const_spec((1, DIFF_HEAD)),
            _const_spec((1, DIFF_HEAD)),
            _const_spec((PAIR, PAIR)),
        ],
        out_specs=pl.BlockSpec((1, tq, DIFF_WIDTH), lambda b, i, j: (b, i, 0)),
        out_shape=jax.ShapeDtypeStruct((bsz, length, DIFF_WIDTH), BF16),
        scratch_shapes=[pltpu.VMEM((N_MAPS, 8, tq), F32), pltpu.VMEM((N_MAPS, PAIR + ONES_ROWS, tq), F32)],
        compiler_params=_cparams(("parallel", "parallel", "arbitrary")),
        name="diff_attn",
    )(k, qt, vt, km, vtm, lq1, lk1, lq2, lk2, subln_g)


def _out_kernel(x_ref, yf_ref, yb_ref, bonus_ref, g_ref, od_ref, lnw_ref, lnb_ref, gs_ref,
                wo_ref, g2n_ref, wg_ref, wu_ref, wd_ref, o_ref):
    gs = gs_ref[...]
    y = yf_ref[0] + yb_ref[0]
    mu = _split_dot(y, gs, 2) * (1.0 / RWKV_HEAD)
    yc = y - mu
    var = _split_dot(yc * yc, gs, 2) * (1.0 / RWKV_HEAD)
    yn = yc * lax.rsqrt(var + GN_EPS) * lnw_ref[...] + lnb_ref[...]
    o_rwkv = (yn + bonus_ref[0].astype(F32)) * g_ref[0].astype(F32)
    x1 = (x_ref[0]
          + jnp.dot(o_rwkv.astype(BF16), wo_ref[:RWKV_WIDTH, :], preferred_element_type=F32)
          + jnp.dot(od_ref[0], wo_ref[RWKV_WIDTH:, :], preferred_element_type=F32))
    ms = jnp.mean(x1 * x1, axis=-1, keepdims=True)
    h2 = (x1 * lax.rsqrt(ms + RMS_EPS) * g2n_ref[...]).astype(BF16)
    gate = jnp.dot(h2, wg_ref[...], preferred_element_type=F32)
    up = jnp.dot(h2, wu_ref[...], preferred_element_type=F32)
    act = (gate * jax.nn.sigmoid(gate) * up).astype(BF16)
    o_ref[0] = x1 + jnp.dot(act, wd_ref[...], preferred_element_type=F32)


def _out(x, yf, yb, bonus, g, od, consts, tm):
    bsz, length, _ = x.shape
    lnw, lnb, gsum, wo, g2n, wg, wu, wd = consts
    tok = lambda b, i: (b, i, 0)
    half = pl.BlockSpec((1, tm, RWKV_WIDTH), tok)
    return pl.pallas_call(
        _out_kernel,
        grid=(bsz, length // tm),
        in_specs=[
            pl.BlockSpec((1, tm, D_MODEL), tok), half, half, half, half, half,
            _const_spec((1, RWKV_WIDTH)),
            _const_spec((1, RWKV_WIDTH)),
            _const_spec((PAIR, PAIR)),
            _const_spec((D_MODEL, D_MODEL)),
            _const_spec((1, D_MODEL)),
            _const_spec((D_MODEL, D_FF)),
            _const_spec((D_MODEL, D_FF)),
            _const_spec((D_FF, D_MODEL)),
        ],
        out_specs=pl.BlockSpec((1, tm, D_MODEL), tok),
        out_shape=jax.ShapeDtypeStruct((bsz, length, D_MODEL), F32),
        compiler_params=_cparams(("parallel", "parallel")),
        name="out_ffn",
    )(x, yf, yb, bonus, g, od, lnw, lnb, gsum, wo, g2n, wg, wu, wd)


def _rope_tables(length, offset, tm):
    inv = ROPE_THETA ** (-jnp.arange(0, DIFF_HEAD, 2, dtype=F32) / DIFF_HEAD)

    def tables(pos):
        ang = pos[:, None] * inv[None, :]
        sin = jnp.sin(ang)
        return jnp.concatenate([jnp.cos(ang)] * 4, axis=-1), jnp.concatenate([-sin, sin, -sin, sin], axis=-1)

    cos_a, sin_a = tables(jnp.arange(0, length, tm, dtype=F32) + offset)
    cos_b, sin_b = tables(jnp.arange(tm, dtype=F32))
    return cos_a[:, None, :], sin_a[:, None, :], cos_b, sin_b


def _block_diag2(a, b):
    za = jnp.zeros_like(a)
    return jnp.concatenate([jnp.concatenate([a, za], axis=1), jnp.concatenate([za, b], axis=1)], axis=0)


def _tile_for(length, pref):
    t = pref
    while length % t:
        t //= 2
    return t


def kernel(x_prompt, x_sample, meta_tokens, norm1_g, w_in, shift_mu_prev, shift_mu_next, w0_f, w2_f, w0_b, w2_b, a0_f, a2_f, a0_b, a2_b, g2, k_k, k_a, r_k, ln_x_w, ln_x_b, q_norm_g, k_norm_g, lam_q1, lam_k1, lam_q2, lam_k2, subln_g, w_out, norm2_g, w_gate, w_up, w_down):
    lam_init = 0.8 - 0.6 * math.exp(-0.3 * 0)
    row = lambda t: t.reshape(1, -1).astype(F32)
    gi = jnp.arange(PAIR) // RWKV_HEAD
    gsum = (gi[:, None] == gi[None, :]).astype(BF16)

    g1 = row(norm1_g[0])
    w_in_b = w_in[0].astype(BF16)
    qg = row(jnp.tile(q_norm_g[0], N_MAPS))
    kg = row(jnp.tile(k_norm_g[0], N_MAPS))
    prep_consts = (
        row(shift_mu_prev[0]), row(shift_mu_next[0]),
        row(jnp.concatenate([w0_f[0], w0_b[0]])), jnp.stack(_split2(_block_diag2(w2_f[0], w2_b[0]).astype(F32))),
        row(jnp.concatenate([a0_f[0], a0_b[0]])), jnp.stack(_split2(_block_diag2(a2_f[0], a2_b[0]).astype(F32))),
        g2[0].astype(BF16), row(k_k[0]), row(k_a[0]), row(r_k[0]), gsum,
    )
    lam_params = (row(lam_q1[0]), row(lam_k1[0]), row(lam_q2[0]), row(lam_k2[0]))
    sg = jnp.broadcast_to(subln_g[0].astype(F32)[:, None], (PAIR, PAIR))
    out_consts = (row(ln_x_w[0]), row(ln_x_b[0]), gsum, w_out[0].astype(BF16), row(norm2_g[0]),
                  w_gate[0].astype(BF16), w_up[0].astype(BF16), w_down[0].astype(BF16))

    meta_x = jnp.zeros((1, META_PAD, D_MODEL), F32).at[0, :N_META].set(meta_tokens.astype(F32))
    p_meta, _, k_meta, vt_meta = _inproj(meta_x, _rope_tables(META_PAD, 0, META_PAD), g1, w_in_b, qg, kg, gsum,
                                         META_PAD)
    prev8 = p_meta[0, N_META - 8:N_META]
    zero8 = jnp.zeros_like(prev8)

    def group(x):
        bsz, length, _ = x.shape
        tm = _tile_for(length, 512)
        p, qt, k, vt = _inproj(x, _rope_tables(length, N_META, tm), g1, w_in_b, qg, kg, gsum, tm)
        pm = jnp.concatenate(
            [jnp.broadcast_to(p_meta[:, :N_META], (bsz, N_META, SHIFT_COLS)),
             p[:, :CHUNK - N_META]], axis=1)
        meta_prep = _prep(pm, zero8, prep_consts, CHUNK, N_META)
        zero_state = jnp.zeros((bsz, N_PAIRS, RWKV_HEAD, PAIR), F32)
        s_meta = _scan(meta_prep, zero_state, 1, False)[2]
        prep_out = _prep(p, prev8, prep_consts, _tile_for(length, 512), None)
        yf, yb, _ = _scan(prep_out, s_meta, 8, True)
        od = _attention(k, qt, vt, k_meta[:, :N_META], vt_meta[:, :, :N_META], lam_params, sg, lam_init,
                        _tile_for(length, 1024), _tile_for(length, 1024))
        return _out(x, yf, yb, prep_out[10], prep_out[9], od, out_consts, _tile_for(length, 512))

    return (group(x_prompt), group(x_sample))
```
